```python
import math
import jax, jax.numpy as jnp
from jax import lax
import numpy as np

D_MODEL = 4096
BATCH = 32
SEQ = 256
DEPTH = 2
DEC_BATCH = 2
DEC_SEQ = 4096
PAST_LEN = 256

GRID_W = 64
HEAD_DIM = 128
POOL_WINDOWS = (2, 4, 8, 16)
N_POOL = 4
POOL_GROUP = 256
POOL_WIDTH = N_POOL * POOL_GROUP
B_Q_HEADS = 24
B_KV_HEADS = 8
B_GROUPS = B_Q_HEADS // B_KV_HEADS
B_Q_WIDTH = B_Q_HEADS * HEAD_DIM
B_KV_WIDTH = B_KV_HEADS * HEAD_DIM
IN0_WIDTH = POOL_WIDTH + B_Q_WIDTH + 2 * B_KV_WIDTH
MIX0_WIDTH = POOL_WIDTH + B_Q_WIDTH
ROPE_THETA = 10000.0
ROPE_FREQS = HEAD_DIM // 4
C_HEADS = 32
C_WIDTH = C_HEADS * HEAD_DIM
NA_ROWS = 8
NA_COLS = 16
D_FF = 11008
N_EXPERTS = 8
TOP_K = 2
D_FF_EXPERT = 14336
EPS = 1e-6
Q_BLOCK = 128
NEG_INF = -1e30

kernel_name = "hybrid_diffusion_prefix_step"


def _rmsnorm(x, g):
    xf = x.astype(jnp.float32)
    xf = xf * lax.rsqrt(jnp.mean(xf * xf, axis=-1, keepdims=True) + EPS)
    return (xf * g.astype(jnp.float32)).astype(x.dtype)


def _modulation(cvec, w, b):
    m = (jax.nn.silu(cvec) @ w + b).reshape(cvec.shape[0], 1, 6, D_MODEL)
    return [m[:, :, i] for i in range(6)]


def _adaln(x, g, shift, scale):
    return _rmsnorm(x, g) * (1 + scale) + shift


def _axial_angles(n):
    t = jnp.arange(n)
    pos = jnp.stack([t // GRID_W, t % GRID_W], axis=-1).astype(jnp.float32)
    inv = ROPE_THETA ** (-jnp.arange(ROPE_FREQS, dtype=jnp.float32) / ROPE_FREQS)
    return pos[:, :, None] * inv


def _apply_rope(x, ang):
    b, n, h, d = x.shape
    xf = x.astype(jnp.float32).reshape(b, n, h, 2, 2, ROPE_FREQS)
    cos = jnp.cos(ang)[None, :, None]
    sin = jnp.sin(ang)[None, :, None]
    x1, x2 = xf[..., 0, :], xf[..., 1, :]
    out = jnp.stack([x1 * cos - x2 * sin, x2 * cos + x1 * sin], axis=-2)
    return out.reshape(b, n, h, d).astype(x.dtype)


def _attend_blocked(q, k, v):
    b, s, kvh, g, d = q.shape
    nb = s // Q_BLOCK
    qb = jnp.moveaxis(q.reshape(b, nb, Q_BLOCK, kvh, g, d), 1, 0)
    scale = d ** -0.5

    def one(qi):
        sc = jnp.einsum('bqhgd,bkhd->bhgqk', qi, k, preferred_element_type=jnp.float32) * scale
        p = jax.nn.softmax(sc, axis=-1).astype(v.dtype)
        return jnp.einsum('bhgqk,bkhd->bqhgd', p, v)

    o = lax.map(one, qb)
    return jnp.moveaxis(o, 0, 1).reshape(b, s, kvh, g, d)


def _pool_mix(u, w_pool, pool_scale):
    b, n, _ = u.shape
    uf = u.astype(jnp.float32).reshape(b, n, N_POOL, POOL_GROUP)
    cs = jnp.concatenate([jnp.zeros((b, 1, N_POOL, POOL_GROUP), jnp.float32),
                          jnp.cumsum(uf, axis=1)], axis=1)
    t = jnp.arange(n)[:, None]
    win = jnp.array(POOL_WINDOWS, jnp.int32)[None, :]
    lo = jnp.clip(t - win // 2, 0, n)
    hi = jnp.clip(t + win - win // 2, 0, n)
    grp = jnp.arange(N_POOL)[None, :]
    cnt = (hi - lo).astype(jnp.float32)[None, :, :, None]
    mean = (cs[:, hi, grp] - cs[:, lo, grp]) / cnt
    diff = (mean - uf).astype(u.dtype)
    y = jnp.einsum('bngc,gcd->bngd', diff, w_pool)
    return y.reshape(b, n, POOL_WIDTH) * pool_scale


def _na_attend(q, k, v, k_ctx, v_ctx, rpb):
    b, n, h, d = q.shape
    rows = n // GRID_W
    wr = min(NA_ROWS, rows)
    t_ctx = k_ctx.shape[1]
    qg = jnp.moveaxis(q.reshape(b, rows, GRID_W, h, d), 1, 0)
    kg = k.reshape(b, rows, GRID_W, h, d)
    vg = v.reshape(b, rows, GRID_W, h, d)
    col = jnp.arange(GRID_W)
    cstart = jnp.clip(col - NA_COLS // 2, 0, GRID_W - NA_COLS)
    col_valid = (col[None, :] >= cstart[:, None]) & (col[None, :] < cstart[:, None] + NA_COLS)
    col_idx = jnp.clip(col[None, :] - col[:, None] + NA_COLS - 1, 0, 2 * NA_COLS - 2)
    scale = d ** -0.5

    def one(args):
        r, qr = args
        rs = jnp.clip(r - wr // 2, 0, rows - wr)
        kr = lax.dynamic_slice_in_dim(kg, rs, wr, axis=1).reshape(b, wr * GRID_W, h, d)
        vr = lax.dynamic_slice_in_dim(vg, rs, wr, axis=1).reshape(b, wr * GRID_W, h, d)
        row_idx = rs + jnp.arange(wr) - r + NA_ROWS - 1
        bias = rpb[:, row_idx[:, None, None], col_idx[None]].astype(jnp.float32)
        bias = jnp.where(col_valid[None, None], bias, NEG_INF)
        bias = jnp.transpose(bias, (0, 2, 1, 3)).reshape(h, GRID_W, wr * GRID_W)
        s_loc = jnp.einsum('bqhd,bkhd->bhqk', qr, kr, preferred_element_type=jnp.float32) * scale + bias[None]
        s_ctx = jnp.einsum('bqhd,bkhd->bhqk', qr, k_ctx, preferred_element_type=jnp.float32) * scale
        p = jax.nn.softmax(jnp.concatenate([s_ctx, s_loc], axis=-1), axis=-1).astype(v.dtype)
        return (jnp.einsum('bhqk,bkhd->bqhd', p[..., :t_ctx], v_ctx)
                + jnp.einsum('bhqk,bkhd->bqhd', p[..., t_ctx:], vr))

    o = lax.map(one, (jnp.arange(rows), qg))
    return jnp.moveaxis(o, 0, 1).reshape(b, n, h, d)


def _mixer_ab(h, w_in, q_norm, k_norm, w_pool, pool_scale, w_out, ctx_k=None, ctx_v=None):
    b, n, _ = h.shape
    u = h @ w_in
    u_pool = u[..., :POOL_WIDTH]
    o0 = POOL_WIDTH
    q = u[..., o0:o0 + B_Q_WIDTH].reshape(b, n, B_Q_HEADS, HEAD_DIM)
    k = u[..., o0 + B_Q_WIDTH:o0 + B_Q_WIDTH + B_KV_WIDTH].reshape(b, n, B_KV_HEADS, HEAD_DIM)
    v = u[..., o0 + B_Q_WIDTH + B_KV_WIDTH:].reshape(b, n, B_KV_HEADS, HEAD_DIM)
    q = _rmsnorm(q, q_norm)
    k = _rmsnorm(k, k_norm)
    if ctx_k is None:
        keys, vals = k, v
    else:
        ang = _axial_angles(n)
        q = _apply_rope(q, ang)
        keys = jnp.concatenate([ctx_k, _apply_rope(k, ang)], axis=1)
        vals = jnp.concatenate([ctx_v, v], axis=1)
    att = _attend_blocked(q.reshape(b, n, B_KV_HEADS, B_GROUPS, HEAD_DIM), keys, vals).reshape(b, n, B_Q_WIDTH)
    pool = _pool_mix(u_pool, w_pool, pool_scale)
    y = jnp.concatenate([pool, att], axis=-1) @ w_out
    return y, k, v


def _mixer_c(h, w_qkv, rpb, w_out, ctx_k=None, ctx_v=None):
    b, n, _ = h.shape
    qkv = (h @ w_qkv).reshape(b, n, 3, C_HEADS, HEAD_DIM)
    q, k, v = qkv[:, :, 0], qkv[:, :, 1], qkv[:, :, 2]
    if ctx_k is None:
        o = _attend_blocked(q[:, :, :, None], k, v)[:, :, :, 0]
    else:
        o = _na_attend(q, k, v, ctx_k, ctx_v, rpb)
    return o.reshape(b, n, C_WIDTH) @ w_out, k, v


def _swiglu(h, w1, w3, w2):
    return (jax.nn.silu(h @ w1) * (h @ w3)) @ w2


def _moe(h, router, w1, w3, w2):
    b, n, d = h.shape
    t = h.reshape(b * n, d)
    logits = (t @ router).astype(jnp.float32)
    top_v, top_i = lax.top_k(logits, TOP_K)
    top_w = jax.nn.softmax(top_v, axis=-1)
    combine = jnp.sum(jax.nn.one_hot(top_i, N_EXPERTS, dtype=jnp.float32) * top_w[..., None], axis=1)
    y = jnp.zeros_like(t)
    for e in range(N_EXPERTS):
        y = y + combine[:, e:e + 1].astype(t.dtype) * _swiglu(t, w1[e], w3[e], w2[e])
    return y.reshape(b, n, d)


def _even_layer(x, cvec, p, ctx_k=None, ctx_v=None):
    (ada_w, ada_b, norm1, w_in, q_norm, k_norm, w_pool, pool_scale, w_out, norm2, w1, w3, w2) = p
    sh1, sc1, g1, sh2, sc2, g2 = _modulation(cvec, ada_w, ada_b)
    y, k, v = _mixer_ab(_adaln(x, norm1, sh1, sc1), w_in, q_norm, k_norm, w_pool, pool_scale, w_out, ctx_k, ctx_v)
    x = x + g1 * y
    x = x + g2 * _swiglu(_adaln(x, norm2, sh2, sc2), w1, w3, w2)
    return x, k, v


def _odd_layer(x, cvec, p, ctx_k=None, ctx_v=None):
    (ada_w, ada_b, norm1, w_qkv, rpb, w_out, norm2, router, w1, w3, w2) = p
    sh1, sc1, g1, sh2, sc2, g2 = _modulation(cvec, ada_w, ada_b)
    y, k, v = _mixer_c(_adaln(x, norm1, sh1, sc1), w_qkv, rpb, w_out, ctx_k, ctx_v)
    x = x + g1 * y
    x = x + g2 * _moe(_adaln(x, norm2, sh2, sc2), router, w1, w3, w2)
    return x, k, v


def setup_inputs(seed: int = 0) -> dict:
    key = jax.random.key(seed)
    ks = iter(jax.random.split(key, 48))

    def nrm(shape, scale):
        return jax.random.normal(next(ks), shape, jnp.float32) * scale

    def gain(shape):
        return 1.0 + nrm(shape, 0.02)

    D = D_MODEL
    return {
        "x_prompt": nrm((BATCH, SEQ, D), 1.0),
        "x_sample": nrm((DEC_BATCH, DEC_SEQ, D), 1.0),
        "c": nrm((DEC_BATCH, D), 1.0),
        "c_ctx": nrm((D,), 1.0),
        "cache_l0_attn_k": nrm((DEC_BATCH, PAST_LEN, B_KV_HEADS, HEAD_DIM), 1.0),
        "cache_l0_attn_v": nrm((DEC_BATCH, PAST_LEN, B_KV_HEADS, HEAD_DIM), 1.0),
        "cache_l1_na_k": nrm((DEC_BATCH, PAST_LEN, C_HEADS, HEAD_DIM), 1.0),
        "cache_l1_na_v": nrm((DEC_BATCH, PAST_LEN, C_HEADS, HEAD_DIM), 1.0),
        "l0_ada_w": nrm((D, 6 * D), 0.5 * D ** -0.5),
        "l0_ada_b": nrm((6 * D,), 0.02),
        "l0_norm1": gain((D,)),
        "l0_w_in": nrm((D, IN0_WIDTH), D ** -0.5),
        "l0_q_norm": gain((HEAD_DIM,)),
        "l0_k_norm": gain((HEAD_DIM,)),
        "l0_w_pool": nrm((N_POOL, POOL_GROUP, POOL_GROUP), POOL_GROUP ** -0.5),
        "l0_pool_scale": gain((POOL_WIDTH,)),
        "l0_w_out": nrm((MIX0_WIDTH, D), MIX0_WIDTH ** -0.5),
        "l0_norm2": gain((D,)),
        "l0_ffn_w1": nrm((D, D_FF), D ** -0.5),
        "l0_ffn_w3": nrm((D, D_FF), D ** -0.5),
        "l0_ffn_w2": nrm((D_FF, D), D_FF ** -0.5),
        "l1_ada_w": nrm((D, 6 * D), 0.5 * D ** -0.5),
        "l1_ada_b": nrm((6 * D,), 0.02),
        "l1_norm1": gain((D,)),
        "l1_w_qkv": nrm((D, 3 * C_WIDTH), D ** -0.5),
        "l1_rpb": nrm((C_HEADS, 2 * NA_ROWS - 1, 2 * NA_COLS - 1), 0.5),
        "l1_w_out": nrm((C_WIDTH, D), C_WIDTH ** -0.5),
        "l1_norm2": gain((D,)),
        "l1_router": nrm((D, N_EXPERTS), D ** -0.5),
        "l1_exp_w1": nrm((N_EXPERTS, D, D_FF_EXPERT), D ** -0.5),
        "l1_exp_w3": nrm((N_EXPERTS, D, D_FF_EXPERT), D ** -0.5),
        "l1_exp_w2": nrm((N_EXPERTS, D_FF_EXPERT, D), D_FF_EXPERT ** -0.5),
        "final_norm": gain((D,)),
    }


def reference(x_prompt, x_sample, c, c_ctx, cache_l0_attn_k, cache_l0_attn_v, cache_l1_na_k, cache_l1_na_v,
              l0_ada_w, l0_ada_b, l0_norm1, l0_w_in, l0_q_norm, l0_k_norm, l0_w_pool, l0_pool_scale, l0_w_out,
              l0_norm2, l0_ffn_w1, l0_ffn_w3, l0_ffn_w2,
              l1_ada_w, l1_ada_b, l1_norm1, l1_w_qkv, l1_rpb, l1_w_out, l1_norm2, l1_router,
              l1_exp_w1, l1_exp_w3, l1_exp_w2, final_norm):
    params = (
        (l0_ada_w, l0_ada_b, l0_norm1, l0_w_in, l0_q_norm, l0_k_norm, l0_w_pool, l0_pool_scale, l0_w_out,
         l0_norm2, l0_ffn_w1, l0_ffn_w3, l0_ffn_w2),
        (l1_ada_w, l1_ada_b, l1_norm1, l1_w_qkv, l1_rpb, l1_w_out, l1_norm2, l1_router,
         l1_exp_w1, l1_exp_w3, l1_exp_w2),
    )
    caches = ((cache_l0_attn_k, cache_l0_attn_v), (cache_l1_na_k, cache_l1_na_v))
    layers = (_even_layer, _odd_layer)
    cvec_ctx = c_ctx[None, :]
    xp, xs = x_prompt, x_sample
    new_k, new_v = [], []
    for i in range(DEPTH):
        layer = layers[i % 2]
        xp, k_i, v_i = layer(xp, cvec_ctx, params[i])
        xs, _, _ = layer(xs, c, params[i], caches[i][0], caches[i][1])
        new_k.append(k_i)
        new_v.append(v_i)
    y_prompt = _rmsnorm(xp, final_norm)
    y_sample = _rmsnorm(xs, final_norm)
    return (y_prompt, y_sample, new_k[0], new_v[0], new_k[1], new_v[1])
```

```python
import functools
import math

import jax
import jax.numpy as jnp
from jax import lax
from jax.experimental import pallas as pl
from jax.experimental.pallas import tpu as pltpu

HEAD_DIM = 128
GRID_W = 64
POOL_WINDOWS = (2, 4, 8, 16)
NA_ROWS = 8
NA_COLS = 16
TOP_K = 2
EPS = 1e-6
ROPE_THETA = 10000.0
NEG_INF = -1e30
LANES = 128
POOL_HALO = 16
VMEM_LIMIT_BYTES = 56 * 1024 * 1024

F32 = jnp.float32
BF16 = jnp.bfloat16


def _cparams(sem):
    return pltpu.CompilerParams(dimension_semantics=sem, vmem_limit_bytes=VMEM_LIMIT_BYTES)


def _mm_kernel(grp_ref, nvalid_ref, *refs, nk, mode):
    del grp_ref
    n_w = 2 if mode == "swiglu" else 1
    a_ref = refs[0]
    w_refs = refs[1:1 + n_w]
    n_extra = {"plain": 0, "swiglu": 0, "bias": 1, "rowscale": 1, "gated_res": 2}[mode]
    extra = refs[1 + n_w:1 + n_w + n_extra]
    o_ref = refs[1 + n_w + n_extra]
    acc_refs = refs[2 + n_w + n_extra:]
    m = pl.program_id(1)
    k = pl.program_id(2)

    def epilogue(accs):
        if mode == "plain":
            r = accs[0]
        elif mode == "swiglu":
            g = accs[0]
            r = (g / (1.0 + jnp.exp(-g))) * accs[1]
        elif mode == "bias":
            r = accs[0] + extra[0][...]
        elif mode == "rowscale":
            r = accs[0] * extra[0][...]
        else:
            r = extra[0][...] + extra[1][0] * accs[0]
        o_ref[...] = r.astype(o_ref.dtype)

    @pl.when(m < nvalid_ref[0])
    def _():
        a = a_ref[...]
        parts = [jnp.dot(a, w_ref[0], preferred_element_type=F32) for w_ref in w_refs]
        if nk == 1:
            epilogue(parts)
        else:
            @pl.when(k == 0)
            def _():
                for acc, p in zip(acc_refs, parts):
                    acc[...] = p

            @pl.when(k > 0)
            def _():
                for acc, p in zip(acc_refs, parts):
                    acc[...] += p

            @pl.when(k == nk - 1)
            def _():
                epilogue([acc[...] for acc in acc_refs])

    @pl.when(jnp.logical_and(m >= nvalid_ref[0], k == nk - 1))
    def _():
        o_ref[...] = jnp.zeros(o_ref.shape, o_ref.dtype)


def _matmul(a, ws, *, bm, bn, bk, mode="plain", out_dtype=F32, extra=(), tile_group=None,
            n_valid=None, gate_rows=None, name="mm"):
    M, K = a.shape
    E, _, N = ws[0].shape
    nm, nn, nk = M // bm, N // bn, K // bk
    assert nm * bm == M and nn * bn == N and nk * bk == K, (a.shape, ws[0].shape, bm, bn, bk)
    if tile_group is None:
        tile_group = jnp.zeros((nm,), jnp.int32)
        n_valid = jnp.full((1,), nm, jnp.int32)
    in_specs = [pl.BlockSpec((bm, bk), lambda n, m, k, g, nv: (m, k))]
    for _ in ws:
        in_specs.append(pl.BlockSpec((1, bk, bn), lambda n, m, k, g, nv: (g[m], k, n)))
    if mode == "bias":
        in_specs.append(pl.BlockSpec((1, bn), lambda n, m, k, g, nv: (0, n)))
    elif mode == "rowscale":
        in_specs.append(pl.BlockSpec((bm, 1), lambda n, m, k, g, nv: (m, 0)))
    elif mode == "gated_res":
        in_specs.append(pl.BlockSpec((bm, bn), lambda n, m, k, g, nv: (m, n)))
        in_specs.append(pl.BlockSpec((1, 1, bn), lambda n, m, k, g, nv: ((m * bm) // gate_rows, 0, n)))
    scratch = [] if nk == 1 else [pltpu.VMEM((bm, bn), F32) for _ in ws]
    return pl.pallas_call(
        functools.partial(_mm_kernel, nk=nk, mode=mode),
        out_shape=jax.ShapeDtypeStruct((M, N), out_dtype),
        grid_spec=pltpu.PrefetchScalarGridSpec(
            num_scalar_prefetch=2,
            grid=(nn, nm, nk),
            in_specs=in_specs,
            out_specs=pl.BlockSpec((bm, bn), lambda n, m, k, g, nv: (m, n)),
            scratch_shapes=scratch),
        compiler_params=_cparams(("parallel", "parallel", "arbitrary")),
        name=name,
    )(tile_group, n_valid, a, *ws, *extra)


def _rms(x, g):
    return x * lax.rsqrt(jnp.mean(x * x, axis=-1, keepdims=True) + EPS) * g


def _adaln_kernel(x_ref, g_ref, sc_ref, sh_ref, o_ref):
    y = _rms(x_ref[...], g_ref[...])
    o_ref[...] = (y * (1.0 + sc_ref[0]) + sh_ref[0]).astype(o_ref.dtype)


def _rmsnorm_kernel(x_ref, g_ref, o_ref):
    o_ref[...] = _rms(x_ref[...], g_ref[...]).astype(o_ref.dtype)


def _adaln_router_kernel(x_ref, g_ref, sc_ref, sh_ref, r_ref, o_ref, idx_ref, w_ref, *, n_experts):
    y = _rms(x_ref[...], g_ref[...])
    h = y * (1.0 + sc_ref[0]) + sh_ref[0]
    o_ref[...] = h
    logits = jnp.dot(h, r_ref[...], preferred_element_type=F32, precision=lax.Precision.HIGHEST)
    lane = lax.broadcasted_iota(jnp.int32, logits.shape, 1).astype(F32)
    lg = jnp.where(lane < n_experts, logits, -jnp.inf)
    v1 = jnp.max(lg, axis=-1, keepdims=True)
    i1 = jnp.min(jnp.where(lg == v1, lane, float(LANES)), axis=-1, keepdims=True)
    lg2 = jnp.where(lane == i1, -jnp.inf, lg)
    v2 = jnp.max(lg2, axis=-1, keepdims=True)
    i2 = jnp.min(jnp.where(lg2 == v2, lane, float(LANES)), axis=-1, keepdims=True)
    e = jnp.exp(v2 - v1)
    den = 1.0 + e
    idx_ref[...] = jnp.where(lane == 0, i1, jnp.where(lane == 1, i2, 0.0)).astype(jnp.int32)
    w_ref[...] = jnp.where(lane == 0, 1.0 / den, jnp.where(lane == 1, e / den, 0.0))


def _adaln(x, g, scale, shift, *, group_rows, bt, out_dtype, router=None):
    T, D = x.shape
    grid = (T // bt,)
    x_spec = pl.BlockSpec((bt, D), lambda i: (i, 0))
    g_spec = pl.BlockSpec((1, D), lambda i: (0, 0))
    mod_spec = pl.BlockSpec((1, 1, D), lambda i: ((i * bt) // group_rows, 0, 0))
    if router is None:
        return pl.pallas_call(
            _adaln_kernel, grid=grid, in_specs=[x_spec, g_spec, mod_spec, mod_spec], out_specs=x_spec,
            out_shape=jax.ShapeDtypeStruct((T, D), out_dtype),
            compiler_params=_cparams(("parallel",)), name="adaln",
        )(x, g, scale, shift)
    n_experts = router.shape[1]
    r_pad = jnp.pad(router, ((0, 0), (0, LANES - n_experts)))
    lane_spec = pl.BlockSpec((bt, LANES), lambda i: (i, 0))
    return pl.pallas_call(
        functools.partial(_adaln_router_kernel, n_experts=n_experts), grid=grid,
        in_specs=[x_spec, g_spec, mod_spec, mod_spec, pl.BlockSpec((D, LANES), lambda i: (0, 0))],
        out_specs=[x_spec, lane_spec, lane_spec],
        out_shape=[jax.ShapeDtypeStruct((T, D), F32), jax.ShapeDtypeStruct((T, LANES), jnp.int32),
                   jax.ShapeDtypeStruct((T, LANES), F32)],
        compiler_params=_cparams(("parallel",)), name="adaln_router",
    )(x, g, scale, shift, r_pad)


def _rmsnorm(x, g, *, bt):
    T, D = x.shape
    x_spec = pl.BlockSpec((bt, D), lambda i: (i, 0))
    return pl.pallas_call(
        _rmsnorm_kernel, grid=(T // bt,), in_specs=[x_spec, pl.BlockSpec((1, D), lambda i: (0, 0))],
        out_specs=x_spec, out_shape=jax.ShapeDtypeStruct((T, D), F32),
        compiler_params=_cparams(("parallel",)), name="final_rmsnorm",
    )(x, g)


def _qk_prep_kernel(q0_ref, q1_ref, q2_ref, k_ref, cos_ref, sin_ref, qn_ref, kn_ref,
                    qo_ref, ko_ref, ks_ref):
    cos = cos_ref[...]
    sin = sin_ref[...]
    lane = lax.broadcasted_iota(jnp.int32, cos.shape, 1)
    first_half = (lane % (HEAD_DIM // 2)) < (HEAD_DIM // 4)

    def rope(y):
        partner = jnp.where(first_half, pltpu.roll(y, HEAD_DIM - HEAD_DIM // 4, 1),
                            pltpu.roll(y, HEAD_DIM // 4, 1))
        return y * cos + partner * sin

    q_heads_per_ref = q0_ref.shape[1] // HEAD_DIM
    for r, q_ref in enumerate((q0_ref, q1_ref, q2_ref)):
        for h in range(q_heads_per_ref):
            sl = slice(h * HEAD_DIM, (h + 1) * HEAD_DIM)
            y = _rms(q_ref[:, sl], qn_ref[...])
            col = (r * q_heads_per_ref + h) * HEAD_DIM
            qo_ref[:, col:col + HEAD_DIM] = rope(y).astype(qo_ref.dtype)
    for h in range(k_ref.shape[1] // HEAD_DIM):
        sl = slice(h * HEAD_DIM, (h + 1) * HEAD_DIM)
        y = _rms(k_ref[:, sl], kn_ref[...])
        ks_ref[:, sl] = y
        ko_ref[:, sl] = rope(y).astype(ko_ref.dtype)


def _qk_prep(u, cos, sin, q_norm, k_norm, *, pool_w, q_w, kv_w, bt):
    T = u.shape[0]
    assert q_w == 3 * kv_w and pool_w == kv_w
    cb = kv_w
    row = lambda i: (i, 0)
    in_specs = [pl.BlockSpec((bt, cb), lambda i, j=j: (i, j)) for j in (1, 2, 3, 4)]
    in_specs += [pl.BlockSpec((bt, HEAD_DIM), row), pl.BlockSpec((bt, HEAD_DIM), row),
                 pl.BlockSpec((1, HEAD_DIM), lambda i: (0, 0)), pl.BlockSpec((1, HEAD_DIM), lambda i: (0, 0))]
    return pl.pallas_call(
        _qk_prep_kernel, grid=(T // bt,), in_specs=in_specs,
        out_specs=[pl.BlockSpec((bt, q_w), row), pl.BlockSpec((bt, kv_w), row), pl.BlockSpec((bt, kv_w), row)],
        out_shape=[jax.ShapeDtypeStruct((T, q_w), BF16), jax.ShapeDtypeStruct((T, kv_w), BF16),
                   jax.ShapeDtypeStruct((T, kv_w), F32)],
        compiler_params=_cparams(("parallel",)), name="qk_norm_rope",
    )(u, u, u, u, cos, sin, q_norm, k_norm)


def _attn_kernel(q_ref, k_ref, v_ref, o_ref, *, groups, scale):
    k = k_ref[...].astype(BF16)
    v = v_ref[...].astype(BF16)
    for g in range(groups):
        sl = slice(g * HEAD_DIM, (g + 1) * HEAD_DIM)
        q = q_ref[:, sl].astype(BF16)
        s = lax.dot_general(q, k, (((1,), (1,)), ((), ())), preferred_element_type=F32) * scale
        p = jnp.exp(s - jnp.max(s, axis=-1, keepdims=True))
        l = jnp.sum(p, axis=-1, keepdims=True)
        o = jnp.dot(p.astype(BF16), v, preferred_element_type=F32) / l
        o_ref[:, sl] = o.astype(o_ref.dtype)


def _attention(q, k, v, *, batch, s_len, t_len, kv_heads, groups, bq, q_row0=0, q_col0=0,
               k_row0=0, k_col0=0, v_col0=0):
    qw = groups * HEAD_DIM
    nq = s_len // bq
    assert q_row0 % bq == 0 and q_col0 % qw == 0 and k_row0 % t_len == 0
    assert k_col0 % HEAD_DIM == 0 and v_col0 % HEAD_DIM == 0
    qr, qc, kr, kc, vc = q_row0 // bq, q_col0 // qw, k_row0 // t_len, k_col0 // HEAD_DIM, v_col0 // HEAD_DIM
    return pl.pallas_call(
        functools.partial(_attn_kernel, groups=groups, scale=HEAD_DIM ** -0.5),
        grid=(batch, kv_heads, nq),
        in_specs=[pl.BlockSpec((bq, qw), lambda b, h, i: (qr + b * nq + i, qc + h)),
                  pl.BlockSpec((t_len, HEAD_DIM), lambda b, h, i: (kr + b, kc + h)),
                  pl.BlockSpec((t_len, HEAD_DIM), lambda b, h, i: (kr + b, vc + h))],
        out_specs=pl.BlockSpec((bq, qw), lambda b, h, i: (b * nq + i, h)),
        out_shape=jax.ShapeDtypeStruct((batch * s_len, kv_heads * qw), BF16),
        compiler_params=_cparams(("parallel", "parallel", "arbitrary")), name="attention",
    )(q, k, v)


def _pool_kernel(u_ref, w_ref, s_ref, o_ref, pad_ref, *, seq):
    g = pl.program_id(1)
    x = u_ref[...]
    pad_ref[0:POOL_HALO, :] = jnp.zeros((POOL_HALO, x.shape[1]), F32)
    pad_ref[POOL_HALO + seq:, :] = jnp.zeros((POOL_HALO, x.shape[1]), F32)
    pad_ref[POOL_HALO:POOL_HALO + seq, :] = x
    t = lax.broadcasted_iota(jnp.int32, x.shape, 0)
    for gi, win in enumerate(POOL_WINDOWS):
        @pl.when(g == gi)
        def _(win=win):
            back, fwd = win // 2, win - win // 2
            tot = pad_ref[POOL_HALO - back:POOL_HALO - back + seq, :]
            for j in range(-back + 1, fwd):
                tot = tot + pad_ref[POOL_HALO + j:POOL_HALO + j + seq, :]
            cnt = (jnp.minimum(t + fwd, seq) - jnp.maximum(t - back, 0)).astype(F32)
            diff = (tot / cnt - x).astype(BF16)
            y = jnp.dot(diff, w_ref[0], preferred_element_type=F32) * s_ref[...]
            o_ref[...] = y.astype(o_ref.dtype)


def _pool_mix(u, w_pool, pool_scale, *, n_seq, seq, row0):
    n_groups, cg, _ = w_pool.shape
    assert row0 % seq == 0 and max(POOL_WINDOWS) // 2 <= POOL_HALO and seq % 8 == 0
    r0 = row0 // seq
    return pl.pallas_call(
        functools.partial(_pool_kernel, seq=seq), grid=(n_seq, n_groups),
        in_specs=[pl.BlockSpec((seq, cg), lambda s, g: (r0 + s, g)),
                  pl.BlockSpec((1, cg, cg), lambda s, g: (g, 0, 0)),
                  pl.BlockSpec((1, cg), lambda s, g: (0, g))],
        out_specs=pl.BlockSpec((seq, cg), lambda s, g: (s, g)),
        out_shape=jax.ShapeDtypeStruct((n_seq * seq, n_groups * cg), BF16),
        scratch_shapes=[pltpu.VMEM((seq + 2 * POOL_HALO, cg), F32)],
        compiler_params=_cparams(("parallel", "arbitrary")), name="pool_mix",
    )(u, w_pool, pool_scale)


def _na_kernel(q_ref, k_ref, v_ref, kc_ref, vc_ref, b_ref, o_ref, *, rows_per_blk, key_rows, n_rows, scale):
    blk = pl.program_id(2)
    k_start = jnp.clip(blk * rows_per_blk - (key_rows - rows_per_blk) // 2, 0, n_rows - key_rows)
    tok0 = pl.multiple_of(k_start * GRID_W, GRID_W * 4)
    n_keys = key_rows * GRID_W
    q = q_ref[...].astype(BF16)
    kl = k_ref[pl.ds(tok0, n_keys), :].astype(BF16)
    vl = v_ref[pl.ds(tok0, n_keys), :].astype(BF16)
    kc = kc_ref[...].astype(BF16)
    vc = vc_ref[...].astype(BF16)
    dn = (((1,), (1,)), ((), ()))
    s_loc = lax.dot_general(q, kl, dn, preferred_element_type=F32) * scale + b_ref[0, 0]
    s_ctx = lax.dot_general(q, kc, dn, preferred_element_type=F32) * scale
    m = jnp.maximum(jnp.max(s_loc, axis=-1, keepdims=True), jnp.max(s_ctx, axis=-1, keepdims=True))
    p_loc = jnp.exp(s_loc - m)
    p_ctx = jnp.exp(s_ctx - m)
    l = jnp.sum(p_loc, axis=-1, keepdims=True) + jnp.sum(p_ctx, axis=-1, keepdims=True)
    o = (jnp.dot(p_ctx.astype(BF16), vc, preferred_element_type=F32)
         + jnp.dot(p_loc.astype(BF16), vl, preferred_element_type=F32)) / l
    o_ref[...] = o.astype(o_ref.dtype)


def _na_bias_table(rpb, n_rows, wr, rows_per_blk, key_rows):
    n_blk = n_rows // rows_per_blk
    col = jnp.arange(GRID_W)
    cstart = jnp.clip(col - NA_COLS // 2, 0, GRID_W - NA_COLS)
    col_valid = (col[None, :] >= cstart[:, None]) & (col[None, :] < cstart[:, None] + NA_COLS)
    col_idx = jnp.clip(col[None, :] - col[:, None] + NA_COLS - 1, 0, 2 * NA_COLS - 2)
    tables = []
    for blk in (0, min(1, n_blk - 1), n_blk - 1):
        r = blk * rows_per_blk + jnp.arange(rows_per_blk)
        k_start = min(max(blk * rows_per_blk - (key_rows - rows_per_blk) // 2, 0), n_rows - key_rows)
        kk = k_start + jnp.arange(key_rows)
        rs = jnp.clip(r - wr // 2, 0, n_rows - wr)
        row_valid = (kk[None, :] >= rs[:, None]) & (kk[None, :] < rs[:, None] + wr)
        row_idx = jnp.clip(kk[None, :] - r[:, None] + NA_ROWS - 1, 0, 2 * NA_ROWS - 2)
        b = rpb[:, row_idx[:, None, :, None], col_idx[None, :, None, :]].astype(F32)
        valid = row_valid[:, None, :, None] & col_valid[None, :, None, :]
        b = jnp.where(valid[None], b, NEG_INF)
        tables.append(b.reshape(rpb.shape[0], rows_per_blk * GRID_W, key_rows * GRID_W))
    return jnp.stack(tables)


def _na_attention(qkv, k_ctx, v_ctx, rpb, *, batch, n_tok, heads, row0, t_ctx):
    n_rows = n_tok // GRID_W
    wr = min(NA_ROWS, n_rows)
    rows_per_blk = min(8, n_rows)
    key_rows = min(rows_per_blk + wr, n_rows)
    n_blk = n_rows // rows_per_blk
    bq = rows_per_blk * GRID_W
    assert n_rows % rows_per_blk == 0 and row0 % n_tok == 0 and row0 % bq == 0
    assert rows_per_blk % 4 == 0 and ((key_rows - rows_per_blk) // 2) % 4 == 0
    bias = _na_bias_table(rpb, n_rows, wr, rows_per_blk, key_rows)
    qr0, kr0 = row0 // bq, row0 // n_tok
    pat = lambda i: jnp.where(i == 0, 0, jnp.where(i == n_blk - 1, 2, 1))
    return pl.pallas_call(
        functools.partial(_na_kernel, rows_per_blk=rows_per_blk, key_rows=key_rows, n_rows=n_rows,
                          scale=HEAD_DIM ** -0.5),
        grid=(heads, batch, n_blk),
        in_specs=[pl.BlockSpec((bq, HEAD_DIM), lambda h, b, i: (qr0 + b * n_blk + i, h)),
                  pl.BlockSpec((n_tok, HEAD_DIM), lambda h, b, i: (kr0 + b, heads + h)),
                  pl.BlockSpec((n_tok, HEAD_DIM), lambda h, b, i: (kr0 + b, 2 * heads + h)),
                  pl.BlockSpec((t_ctx, HEAD_DIM), lambda h, b, i: (b, h)),
                  pl.BlockSpec((t_ctx, HEAD_DIM), lambda h, b, i: (b, h)),
                  pl.BlockSpec((1, 1, bq, key_rows * GRID_W), lambda h, b, i: (pat(i), h, 0, 0))],
        out_specs=pl.BlockSpec((bq, HEAD_DIM), lambda h, b, i: (b * n_blk + i, h)),
        out_shape=jax.ShapeDtypeStruct((batch * n_tok, heads * HEAD_DIM), BF16),
        compiler_params=_cparams(("parallel", "parallel", "arbitrary")), name="na_attention",
    )(qkv, qkv, qkv, k_ctx, v_ctx, bias)


def _row_copy(src_hbm, row, dst, slot, sem):
    return pltpu.make_async_copy(src_hbm.at[pl.ds(row, 1), :], dst.at[pl.ds(slot, 1), :], sem)


def _gather_kernel(src_ref, x_hbm, o_ref, buf, sem, *, rt):
    base = pl.program_id(0) * rt

    def issue(r, c):
        _row_copy(x_hbm, src_ref[base + r], buf, r, sem).start()
        return c

    def wait(r, c):
        _row_copy(x_hbm, 0, buf, r, sem).wait()
        return c

    lax.fori_loop(0, rt, issue, 0, unroll=8)
    lax.fori_loop(0, rt, wait, 0, unroll=8)
    o_ref[...] = buf[...].astype(o_ref.dtype)


def _gather_rows(x, src, *, rt):
    n = src.shape[0]
    D = x.shape[1]
    return pl.pallas_call(
        functools.partial(_gather_kernel, rt=rt),
        out_shape=jax.ShapeDtypeStruct((n, D), BF16),
        grid_spec=pltpu.PrefetchScalarGridSpec(
            num_scalar_prefetch=1, grid=(n // rt,),
            in_specs=[pl.BlockSpec(memory_space=pl.ANY)],
            out_specs=pl.BlockSpec((rt, D), lambda i, s: (i, 0)),
            scratch_shapes=[pltpu.VMEM((rt, D), F32), pltpu.SemaphoreType.DMA]),
        compiler_params=_cparams(("arbitrary",)), name="moe_dispatch",
    )(src, x)


def _combine_kernel(pos_ref, x_ref, g_ref, y_hbm, o_ref, buf0, buf1, sem, *, ct):
    base = pl.program_id(0) * ct

    def issue(r, c):
        _row_copy(y_hbm, pos_ref[2 * (base + r)], buf0, r, sem).start()
        _row_copy(y_hbm, pos_ref[2 * (base + r) + 1], buf1, r, sem).start()
        return c

    def wait(r, c):
        _row_copy(y_hbm, 0, buf0, r, sem).wait()
        _row_copy(y_hbm, 0, buf1, r, sem).wait()
        return c

    lax.fori_loop(0, ct, issue, 0, unroll=8)
    lax.fori_loop(0, ct, wait, 0, unroll=8)
    o_ref[...] = x_ref[...] + g_ref[0] * (buf0[...] + buf1[...])


def _combine(x, gate, y, pos, *, ct, gate_rows):
    T, D = x.shape
    return pl.pallas_call(
        functools.partial(_combine_kernel, ct=ct),
        out_shape=jax.ShapeDtypeStruct((T, D), F32),
        grid_spec=pltpu.PrefetchScalarGridSpec(
            num_scalar_prefetch=1, grid=(T // ct,),
            in_specs=[pl.BlockSpec((ct, D), lambda i, p: (i, 0)),
                      pl.BlockSpec((1, 1, D), lambda i, p: ((i * ct) // gate_rows, 0, 0)),
                      pl.BlockSpec(memory_space=pl.ANY)],
            out_specs=pl.BlockSpec((ct, D), lambda i, p: (i, 0)),
            scratch_shapes=[pltpu.VMEM((ct, D), F32), pltpu.VMEM((ct, D), F32), pltpu.SemaphoreType.DMA]),
        compiler_params=_cparams(("arbitrary",)), name="moe_combine",
    )(pos, x, gate, y)


def _routing_plan(top_i, top_w, n_experts, tm):
    T = top_i.shape[0]
    e_flat = top_i.reshape(-1)
    onehot = (e_flat[:, None] == jnp.arange(n_experts)[None, :]).astype(jnp.int32)
    counts = jnp.sum(onehot, axis=0)
    rank = jnp.sum((jnp.cumsum(onehot, axis=0) - onehot) * onehot, axis=1)
    gsz = ((counts + tm - 1) // tm) * tm
    gend = jnp.cumsum(gsz)
    goff = gend - gsz
    pos = (goff[e_flat] + rank).astype(jnp.int32)
    n_tiles = (T * TOP_K) // tm + n_experts
    src = jnp.zeros((n_tiles * tm,), jnp.int32).at[pos].set(jnp.arange(T * TOP_K, dtype=jnp.int32) // TOP_K)
    scale = jnp.zeros((n_tiles * tm,), F32).at[pos].set(top_w.reshape(-1))
    n_valid = (gend[-1] // tm).astype(jnp.int32)
    tile_row = jnp.minimum(jnp.arange(n_tiles, dtype=jnp.int32), n_valid - 1) * tm
    tile_expert = jnp.sum((tile_row[:, None] >= gend[None, :]).astype(jnp.int32), axis=1)
    return pos, src, scale[:, None], tile_expert.astype(jnp.int32), n_valid.reshape(1)


def _pick(n, prefs):
    for p in prefs:
        if n % p == 0:
            return p
    return n


def _modulation(cvecs, w, b):
    n_c, D = cvecs.shape
    rows = 8
    a = jnp.zeros((rows, D), BF16).at[:n_c].set(jax.nn.silu(cvecs).astype(BF16))
    m = _matmul(a, (w.astype(BF16)[None],), bm=rows, bn=_pick(6 * D, (2048, 1024, 512, 256, 128)), bk=D,
                mode="bias", extra=(b[None, :],), name="modulation")
    return jnp.transpose(m[:n_c].reshape(n_c, 6, 1, D), (1, 0, 2, 3))


def _rope_tables(n_prompt_rows, dec_batch, dec_seq):
    t = jnp.arange(dec_seq)
    pos = jnp.stack([t // GRID_W, t % GRID_W], axis=-1).astype(F32)
    n_freq = HEAD_DIM // 4
    inv = ROPE_THETA ** (-jnp.arange(n_freq, dtype=F32) / n_freq)
    ang = pos[:, :, None] * inv
    cos = jnp.concatenate([jnp.cos(ang)] * 2, axis=-1).reshape(dec_seq, HEAD_DIM)
    sin = jnp.concatenate([-jnp.sin(ang), jnp.sin(ang)], axis=-1).reshape(dec_seq, HEAD_DIM)
    cos = jnp.concatenate([jnp.ones((n_prompt_rows, HEAD_DIM), F32)] + [cos] * dec_batch)
    sin = jnp.concatenate([jnp.zeros((n_prompt_rows, HEAD_DIM), F32)] + [sin] * dec_batch)
    return cos, sin


def kernel(x_prompt, x_sample, c, c_ctx, cache_l0_attn_k, cache_l0_attn_v, cache_l1_na_k, cache_l1_na_v,
           l0_ada_w, l0_ada_b, l0_norm1, l0_w_in, l0_q_norm, l0_k_norm, l0_w_pool, l0_pool_scale, l0_w_out,
           l0_norm2, l0_ffn_w1, l0_ffn_w3, l0_ffn_w2,
           l1_ada_w, l1_ada_b, l1_norm1, l1_w_qkv, l1_rpb, l1_w_out, l1_norm2, l1_router,
           l1_exp_w1, l1_exp_w3, l1_exp_w2, final_norm):
    batch, seq, D = x_prompt.shape
    dec_batch, dec_seq, _ = x_sample.shape
    past_len = cache_l0_attn_k.shape[1]
    kv_heads0 = cache_l0_attn_k.shape[2]
    heads1 = cache_l1_na_k.shape[2]
    n_pool, cg, _ = l0_w_pool.shape
    pool_w = n_pool * cg
    kv_w = kv_heads0 * HEAD_DIM
    q_w = l0_w_in.shape[1] - pool_w - 2 * kv_w
    groups0 = q_w // kv_w
    c_w = heads1 * HEAD_DIM
    n_experts = l1_router.shape[1]
    Tp, Ts = batch * seq, dec_batch * dec_seq
    T = Tp + Ts
    group_rows = math.gcd(Tp, dec_seq)
    grp_to_c = jnp.array([0] * (Tp // group_rows) + [1 + b for b in range(dec_batch)
                                                      for _ in range(dec_seq // group_rows)], jnp.int32)

    bt = _pick(group_rows, (256, 128, 64, 32, 16, 8))
    bm = _pick(group_rows, (1024, 512, 256, 128, 64, 32, 16, 8))
    blk = lambda n: _pick(n, (512, 256, 128))
    mods = lambda m: [m[i][grp_to_c] for i in range(6)]
    cvecs = jnp.concatenate([c_ctx[None, :], c], axis=0)
    gg = dict(gate_rows=group_rows)

    x = jnp.concatenate([x_prompt.reshape(Tp, D), x_sample.reshape(Ts, D)], axis=0)

    sh1, sc1, g1, sh2, sc2, g2 = mods(_modulation(cvecs, l0_ada_w, l0_ada_b))
    h = _adaln(x, l0_norm1[None], sc1, sh1, group_rows=group_rows, bt=bt, out_dtype=BF16)
    u = _matmul(h, (l0_w_in.astype(BF16)[None],), bm=bm, bn=blk(l0_w_in.shape[1]), bk=D, name="l0_in_proj")
    cos, sin = _rope_tables(Tp, dec_batch, dec_seq)
    q_att, k_att, k_state = _qk_prep(u, cos, sin, l0_q_norm[None], l0_k_norm[None],
                                     pool_w=pool_w, q_w=q_w, kv_w=kv_w, bt=bt)
    v_col0 = pool_w + q_w + kv_w
    att_p = _attention(q_att, k_att, u, batch=batch, s_len=seq, t_len=seq, kv_heads=kv_heads0, groups=groups0,
                       bq=_pick(seq, (256, 128, 64, 32, 16, 8)), v_col0=v_col0)
    k_lat = jnp.concatenate([cache_l0_attn_k.reshape(dec_batch, past_len, kv_w).astype(BF16),
                             k_att[Tp:].reshape(dec_batch, dec_seq, kv_w)], axis=1).reshape(-1, kv_w)
    v_lat = jnp.concatenate([cache_l0_attn_v.reshape(dec_batch, past_len, kv_w),
                             u[Tp:, v_col0:].reshape(dec_batch, dec_seq, kv_w)], axis=1).reshape(-1, kv_w)
    att_s = _attention(q_att, k_lat, v_lat, batch=dec_batch, s_len=dec_seq, t_len=past_len + dec_seq,
                       kv_heads=kv_heads0, groups=groups0, bq=_pick(dec_seq, (256, 128, 64, 32, 16, 8)), q_row0=Tp)
    w_pool = l0_w_pool.astype(BF16)
    pool_p = _pool_mix(u, w_pool, l0_pool_scale[None], n_seq=batch, seq=seq, row0=0)
    pool_s = _pool_mix(u, w_pool, l0_pool_scale[None], n_seq=dec_batch, seq=dec_seq, row0=Tp)
    mix = jnp.concatenate([jnp.concatenate([pool_p, pool_s], axis=0),
                           jnp.concatenate([att_p, att_s], axis=0)], axis=1)
    x = _matmul(mix, (l0_w_out.astype(BF16)[None],), bm=bm, bn=blk(D), bk=mix.shape[1], mode="gated_res",
                extra=(x, g1), name="l0_out_proj", **gg)
    h = _adaln(x, l0_norm2[None], sc2, sh2, group_rows=group_rows, bt=bt, out_dtype=BF16)
    d_ff = l0_ffn_w1.shape[1]
    hh = _matmul(h, (l0_ffn_w1.astype(BF16)[None], l0_ffn_w3.astype(BF16)[None]), bm=bm, bn=blk(d_ff), bk=D,
                 mode="swiglu", out_dtype=BF16, name="l0_ffn_up")
    x = _matmul(hh, (l0_ffn_w2.astype(BF16)[None],), bm=bm, bn=blk(D), bk=_pick(d_ff, (5504, 2048, 1024, 512, 256, 128)),
                mode="gated_res", extra=(x, g2), name="l0_ffn_down", **gg)
    state_l0_k = k_state[:Tp].reshape(batch, seq, kv_heads0, HEAD_DIM)
    state_l0_v = u[:Tp, v_col0:].reshape(batch, seq, kv_heads0, HEAD_DIM)

    sh1, sc1, g1, sh2, sc2, g2 = mods(_modulation(cvecs, l1_ada_w, l1_ada_b))
    h = _adaln(x, l1_norm1[None], sc1, sh1, group_rows=group_rows, bt=bt, out_dtype=BF16)
    qkv = _matmul(h, (l1_w_qkv.astype(BF16)[None],), bm=bm, bn=blk(3 * c_w), bk=D, name="l1_qkv_proj")
    att_p = _attention(qkv, qkv, qkv, batch=batch, s_len=seq, t_len=seq, kv_heads=heads1, groups=1,
                       bq=_pick(seq, (256, 128, 64, 32, 16, 8)), k_col0=c_w, v_col0=2 * c_w)
    att_s = _na_attention(qkv, cache_l1_na_k.reshape(dec_batch * past_len, c_w),
                          cache_l1_na_v.reshape(dec_batch * past_len, c_w), l1_rpb,
                          batch=dec_batch, n_tok=dec_seq, heads=heads1, row0=Tp, t_ctx=past_len)
    x = _matmul(jnp.concatenate([att_p, att_s], axis=0), (l1_w_out.astype(BF16)[None],), bm=bm, bn=blk(D), bk=c_w,
                mode="gated_res", extra=(x, g1), name="l1_out_proj", **gg)
    h32, top_i, top_w = _adaln(x, l1_norm2[None], sc2, sh2, group_rows=group_rows, bt=bt, out_dtype=F32,
                               router=l1_router)
    tm = _pick(T * TOP_K, (512, 256, 128, 64, 32, 16, 8))
    pos, src, row_scale, tile_expert, n_valid = _routing_plan(top_i[:, :TOP_K], top_w[:, :TOP_K], n_experts, tm)
    xs = _gather_rows(h32, src, rt=_pick(tm, (256, 128, 64, 32, 16, 8)))
    d_fe = l1_exp_w1.shape[2]
    he = _matmul(xs, (l1_exp_w1.astype(BF16), l1_exp_w3.astype(BF16)), bm=tm, bn=blk(d_fe), bk=D, mode="swiglu",
                 out_dtype=BF16, tile_group=tile_expert, n_valid=n_valid, name="moe_up")
    ye = _matmul(he, (l1_exp_w2.astype(BF16),), bm=tm, bn=_pick(D, (1024, 512, 256, 128)),
                 bk=_pick(d_fe, (3584, 2048, 1024, 512, 256, 128)), mode="rowscale", extra=(row_scale,),
                 tile_group=tile_expert, n_valid=n_valid, name="moe_down")
    x = _combine(x, g2, ye, pos, ct=_pick(group_rows, (128, 64, 32, 16, 8)), gate_rows=group_rows)
    state_l1_k = qkv[:Tp, c_w:2 * c_w].reshape(batch, seq, heads1, HEAD_DIM)
    state_l1_v = qkv[:Tp, 2 * c_w:].reshape(batch, seq, heads1, HEAD_DIM)

    y = _rmsnorm(x, final_norm[None], bt=bt)
    return (y[:Tp].reshape(batch, seq, D), y[Tp:].reshape(dec_batch, dec_seq, D),
            state_l0_k, state_l0_v, state_l1_k, state_l1_v)
```

```python
import functools
import math

import jax
import jax.numpy as jnp
from jax import lax
from jax.experimental import pallas as pl
from jax.experimental.pallas import tpu as pltpu

HEAD_DIM = 128
GRID_W = 64
POOL_WINDOWS = (2, 4, 8, 16)
NA_ROWS = 8
NA_COLS = 16
TOP_K = 2
EPS = 1e-6
ROPE_THETA = 10000.0
NEG_INF = -1e30
LANES = 128
POOL_HALO = 16
VMEM_LIMIT_BYTES = 60 * 1024 * 1024

F32 = jnp.float32
BF16 = jnp.bfloat16


def _cparams(sem):
    return pltpu.CompilerParams(dimension_semantics=sem, vmem_limit_bytes=VMEM_LIMIT_BYTES)


def _mm_kernel(grp_ref, nvalid_ref, *refs, nk, mode, cast_w):
    n_w = 2 if mode == "swiglu" else 1
    a_ref = refs[0]
    w_refs = refs[1:1 + n_w]
    n_extra = {"plain": 0, "swiglu": 0, "bias": 1, "rowscale": 1, "gated_res": 2}[mode]
    extra = refs[1 + n_w:1 + n_w + n_extra]
    o_ref = refs[1 + n_w + n_extra]
    scratch = refs[2 + n_w + n_extra:]
    m = pl.program_id(1)
    k = pl.program_id(2)

    def epilogue(accs):
        if mode == "plain":
            r = accs[0]
        elif mode == "swiglu":
            g = accs[0]
            r = (g / (1.0 + jnp.exp(-g))) * accs[1]
        elif mode == "bias":
            r = accs[0] + extra[0][...]
        elif mode == "rowscale":
            r = accs[0] * extra[0][...]
        else:
            r = extra[0][...] + extra[1][0] * accs[0]
        o_ref[...] = r.astype(o_ref.dtype)

    @pl.when(m < nvalid_ref[0])
    def _():
        if cast_w:
            @pl.when(jnp.logical_or(m == 0, grp_ref[m] != grp_ref[jnp.maximum(m - 1, 0)]))
            def _():
                for w_ref, wb in zip(w_refs, scratch):
                    wb[...] = w_ref[0].astype(BF16)
            ws = [wb[...] for wb in scratch]
        else:
            ws = [w_ref[0] for w_ref in w_refs]
        a = a_ref[...]
        parts = [jnp.dot(a, w, preferred_element_type=F32) for w in ws]
        if nk == 1:
            epilogue(parts)
        else:
            @pl.when(k == 0)
            def _():
                for acc, p in zip(scratch, parts):
                    acc[...] = p

            @pl.when(k > 0)
            def _():
                for acc, p in zip(scratch, parts):
                    acc[...] += p

            @pl.when(k == nk - 1)
            def _():
                epilogue([acc[...] for acc in scratch])

    @pl.when(jnp.logical_and(m >= nvalid_ref[0], k == nk - 1))
    def _():
        o_ref[...] = jnp.zeros(o_ref.shape, o_ref.dtype)


def _matmul(a, ws, *, bm, bn, bk, mode="plain", out_dtype=F32, extra=(), tile_group=None,
            n_valid=None, gate_rows=None, name="mm"):
    M, K = a.shape
    E, _, N = ws[0].shape
    nm, nn, nk = M // bm, N // bn, K // bk
    assert nm * bm == M and nn * bn == N and nk * bk == K, (a.shape, ws[0].shape, bm, bn, bk)
    cast_w = ws[0].dtype == F32
    assert not (cast_w and nk > 1)
    if tile_group is None:
        tile_group = jnp.zeros((nm,), jnp.int32)
        n_valid = jnp.full((1,), nm, jnp.int32)
    in_specs = [pl.BlockSpec((bm, bk), lambda n, m, k, g, nv: (m, k))]
    for _ in ws:
        in_specs.append(pl.BlockSpec((1, bk, bn), lambda n, m, k, g, nv: (g[m], k, n)))
    if mode == "bias":
        in_specs.append(pl.BlockSpec((1, bn), lambda n, m, k, g, nv: (0, n)))
    elif mode == "rowscale":
        in_specs.append(pl.BlockSpec((bm, 1), lambda n, m, k, g, nv: (m, 0)))
    elif mode == "gated_res":
        in_specs.append(pl.BlockSpec((bm, bn), lambda n, m, k, g, nv: (m, n)))
        in_specs.append(pl.BlockSpec((1, 1, bn), lambda n, m, k, g, nv: ((m * bm) // gate_rows, 0, n)))
    if cast_w:
        scratch = [pltpu.VMEM((bk, bn), BF16) for _ in ws]
    else:
        scratch = [] if nk == 1 else [pltpu.VMEM((bm, bn), F32) for _ in ws]
    return pl.pallas_call(
        functools.partial(_mm_kernel, nk=nk, mode=mode, cast_w=cast_w),
        out_shape=jax.ShapeDtypeStruct((M, N), out_dtype),
        grid_spec=pltpu.PrefetchScalarGridSpec(
            num_scalar_prefetch=2,
            grid=(nn, nm, nk),
            in_specs=in_specs,
            out_specs=pl.BlockSpec((bm, bn), lambda n, m, k, g, nv: (m, n)),
            scratch_shapes=scratch),
        compiler_params=_cparams(("arbitrary", "arbitrary", "arbitrary")),
        name=name,
    )(tile_group, n_valid, a, *ws, *extra)


def _rms(x, g):
    return x * lax.rsqrt(jnp.mean(x * x, axis=-1, keepdims=True) + EPS) * g


def _adaln_kernel(x_ref, g_ref, sc_ref, sh_ref, o_ref):
    y = _rms(x_ref[...], g_ref[...])
    o_ref[...] = (y * (1.0 + sc_ref[0]) + sh_ref[0]).astype(o_ref.dtype)


def _rmsnorm_kernel(x_ref, g_ref, o_ref):
    o_ref[...] = _rms(x_ref[...], g_ref[...]).astype(o_ref.dtype)


def _adaln_router_kernel(x_ref, g_ref, sc_ref, sh_ref, r_ref, o_ref, idx_ref, w_ref, *, n_experts):
    y = _rms(x_ref[...], g_ref[...])
    h = y * (1.0 + sc_ref[0]) + sh_ref[0]
    o_ref[...] = h
    logits = jnp.dot(h, r_ref[...], preferred_element_type=F32, precision=lax.Precision.HIGHEST)
    lane = lax.broadcasted_iota(jnp.int32, logits.shape, 1).astype(F32)
    lg = jnp.where(lane < n_experts, logits, -jnp.inf)
    v1 = jnp.max(lg, axis=-1, keepdims=True)
    i1 = jnp.min(jnp.where(lg == v1, lane, float(LANES)), axis=-1, keepdims=True)
    lg2 = jnp.where(lane == i1, -jnp.inf, lg)
    v2 = jnp.max(lg2, axis=-1, keepdims=True)
    i2 = jnp.min(jnp.where(lg2 == v2, lane, float(LANES)), axis=-1, keepdims=True)
    e = jnp.exp(v2 - v1)
    den = 1.0 + e
    idx_ref[...] = jnp.where(lane == 0, i1, jnp.where(lane == 1, i2, 0.0)).astype(jnp.int32)
    w_ref[...] = jnp.where(lane == 0, 1.0 / den, jnp.where(lane == 1, e / den, 0.0))


def _adaln(x, g, scale, shift, *, group_rows, bt, out_dtype, router=None):
    T, D = x.shape
    grid = (T // bt,)
    x_spec = pl.BlockSpec((bt, D), lambda i: (i, 0))
    g_spec = pl.BlockSpec((1, D), lambda i: (0, 0))
    mod_spec = pl.BlockSpec((1, 1, D), lambda i: ((i * bt) // group_rows, 0, 0))
    if router is None:
        return pl.pallas_call(
            _adaln_kernel, grid=grid, in_specs=[x_spec, g_spec, mod_spec, mod_spec], out_specs=x_spec,
            out_shape=jax.ShapeDtypeStruct((T, D), out_dtype),
            compiler_params=_cparams(("parallel",)), name="adaln",
        )(x, g, scale, shift)
    n_experts = router.shape[1]
    r_pad = jnp.pad(router, ((0, 0), (0, LANES - n_experts)))
    lane_spec = pl.BlockSpec((bt, LANES), lambda i: (i, 0))
    return pl.pallas_call(
        functools.partial(_adaln_router_kernel, n_experts=n_experts), grid=grid,
        in_specs=[x_spec, g_spec, mod_spec, mod_spec, pl.BlockSpec((D, LANES), lambda i: (0, 0))],
        out_specs=[x_spec, lane_spec, lane_spec],
        out_shape=[jax.ShapeDtypeStruct((T, D), F32), jax.ShapeDtypeStruct((T, LANES), jnp.int32),
                   jax.ShapeDtypeStruct((T, LANES), F32)],
        compiler_params=_cparams(("parallel",)), name="adaln_router",
    )(x, g, scale, shift, r_pad)


def _rmsnorm(x, g, *, bt, row0, n_rows):
    D = x.shape[1]
    r0 = row0 // bt
    assert r0 * bt == row0 and n_rows % bt == 0
    return pl.pallas_call(
        _rmsnorm_kernel, grid=(n_rows // bt,),
        in_specs=[pl.BlockSpec((bt, D), lambda i: (r0 + i, 0)), pl.BlockSpec((1, D), lambda i: (0, 0))],
        out_specs=pl.BlockSpec((bt, D), lambda i: (i, 0)), out_shape=jax.ShapeDtypeStruct((n_rows, D), F32),
        compiler_params=_cparams(("parallel",)), name="final_rmsnorm",
    )(x, g)


def _qk_prep_kernel(q0_ref, q1_ref, q2_ref, k_ref, cos_ref, sin_ref, qn_ref, kn_ref,
                    qo_ref, ko_ref, ks_ref):
    cos = cos_ref[...]
    sin = sin_ref[...]
    lane = lax.broadcasted_iota(jnp.int32, cos.shape, 1)
    first_half = (lane % (HEAD_DIM // 2)) < (HEAD_DIM // 4)

    def rope(y):
        partner = jnp.where(first_half, pltpu.roll(y, HEAD_DIM - HEAD_DIM // 4, 1),
                            pltpu.roll(y, HEAD_DIM // 4, 1))
        return y * cos + partner * sin

    q_heads_per_ref = q0_ref.shape[1] // HEAD_DIM
    for r, q_ref in enumerate((q0_ref, q1_ref, q2_ref)):
        for h in range(q_heads_per_ref):
            sl = slice(h * HEAD_DIM, (h + 1) * HEAD_DIM)
            y = _rms(q_ref[:, sl], qn_ref[...])
            col = (r * q_heads_per_ref + h) * HEAD_DIM
            qo_ref[:, col:col + HEAD_DIM] = rope(y).astype(qo_ref.dtype)
    for h in range(k_ref.shape[1] // HEAD_DIM):
        sl = slice(h * HEAD_DIM, (h + 1) * HEAD_DIM)
        y = _rms(k_ref[:, sl], kn_ref[...])
        ks_ref[:, sl] = y
        ko_ref[:, sl] = rope(y).astype(ko_ref.dtype)


def _qk_prep(u, cos, sin, q_norm, k_norm, *, pool_w, q_w, kv_w, bt):
    T = u.shape[0]
    assert q_w == 3 * kv_w and pool_w == kv_w
    cb = kv_w
    row = lambda i: (i, 0)
    in_specs = [pl.BlockSpec((bt, cb), lambda i, j=j: (i, j)) for j in (1, 2, 3, 4)]
    in_specs += [pl.BlockSpec((bt, HEAD_DIM), row), pl.BlockSpec((bt, HEAD_DIM), row),
                 pl.BlockSpec((1, HEAD_DIM), lambda i: (0, 0)), pl.BlockSpec((1, HEAD_DIM), lambda i: (0, 0))]
    return pl.pallas_call(
        _qk_prep_kernel, grid=(T // bt,), in_specs=in_specs,
        out_specs=[pl.BlockSpec((bt, q_w), row), pl.BlockSpec((bt, kv_w), row), pl.BlockSpec((bt, kv_w), row)],
        out_shape=[jax.ShapeDtypeStruct((T, q_w), BF16), jax.ShapeDtypeStruct((T, kv_w), BF16),
                   jax.ShapeDtypeStruct((T, kv_w), F32)],
        compiler_params=_cparams(("parallel",)), name="qk_norm_rope",
    )(u, u, u, u, cos, sin, q_norm, k_norm)


def _attn_kernel(q_ref, k_ref, v_ref, o_ref, *, heads, groups, scale):
    for h in range(heads):
        kv_sl = slice(h * HEAD_DIM, (h + 1) * HEAD_DIM)
        k = k_ref[:, kv_sl].astype(BF16)
        v = v_ref[:, kv_sl].astype(BF16)
        for g in range(groups):
            c = (h * groups + g) * HEAD_DIM
            q = q_ref[:, c:c + HEAD_DIM].astype(BF16)
            s = lax.dot_general(q, k, (((1,), (1,)), ((), ())), preferred_element_type=F32) * scale
            p = jnp.exp(s - jnp.max(s, axis=-1, keepdims=True))
            l = jnp.sum(p, axis=-1, keepdims=True)
            o = jnp.dot(p.astype(BF16), v, preferred_element_type=F32) / l
            o_ref[:, c:c + HEAD_DIM] = o.astype(o_ref.dtype)


def _attention(q, k, v, *, batch, s_len, t_len, kv_heads, groups, bq, hp, q_row0=0, q_col0=0,
               k_row0=0, k_col0=0, v_col0=0):
    qw, kw = hp * groups * HEAD_DIM, hp * HEAD_DIM
    nq = s_len // bq
    assert q_row0 % bq == 0 and q_col0 % qw == 0 and k_row0 % t_len == 0 and kv_heads % hp == 0
    assert k_col0 % kw == 0 and v_col0 % kw == 0
    qr, qc, kr, kc, vc = q_row0 // bq, q_col0 // qw, k_row0 // t_len, k_col0 // kw, v_col0 // kw
    return pl.pallas_call(
        functools.partial(_attn_kernel, heads=hp, groups=groups, scale=HEAD_DIM ** -0.5),
        grid=(batch, kv_heads // hp, nq),
        in_specs=[pl.BlockSpec((bq, qw), lambda b, h, i: (qr + b * nq + i, qc + h)),
                  pl.BlockSpec((t_len, kw), lambda b, h, i: (kr + b, kc + h)),
                  pl.BlockSpec((t_len, kw), lambda b, h, i: (kr + b, vc + h))],
        out_specs=pl.BlockSpec((bq, qw), lambda b, h, i: (b * nq + i, h)),
        out_shape=jax.ShapeDtypeStruct((batch * s_len, kv_heads * groups * HEAD_DIM), BF16),
        compiler_params=_cparams(("parallel", "parallel", "arbitrary")), name="attention",
    )(q, k, v)


def _pool_kernel(u_ref, w_ref, s_ref, o_ref, pad_ref, *, seq):
    g = pl.program_id(1)
    x = u_ref[...]
    pad_ref[0:POOL_HALO, :] = jnp.zeros((POOL_HALO, x.shape[1]), F32)
    pad_ref[POOL_HALO + seq:, :] = jnp.zeros((POOL_HALO, x.shape[1]), F32)
    pad_ref[POOL_HALO:POOL_HALO + seq, :] = x
    t = lax.broadcasted_iota(jnp.int32, x.shape, 0)
    for gi, win in enumerate(POOL_WINDOWS):
        @pl.when(g == gi)
        def _(win=win):
            back, fwd = win // 2, win - win // 2
            tot = pad_ref[POOL_HALO - back:POOL_HALO - back + seq, :]
            for j in range(-back + 1, fwd):
                tot = tot + pad_ref[POOL_HALO + j:POOL_HALO + j + seq, :]
            cnt = (jnp.minimum(t + fwd, seq) - jnp.maximum(t - back, 0)).astype(F32)
            diff = (tot / cnt - x).astype(BF16)
            y = jnp.dot(diff, w_ref[0], preferred_element_type=F32) * s_ref[...]
            o_ref[...] = y.astype(o_ref.dtype)


def _pool_mix(u, w_pool, pool_scale, *, n_seq, seq, row0):
    n_groups, cg, _ = w_pool.shape
    assert row0 % seq == 0 and max(POOL_WINDOWS) // 2 <= POOL_HALO and seq % 8 == 0
    r0 = row0 // seq
    return pl.pallas_call(
        functools.partial(_pool_kernel, seq=seq), grid=(n_seq, n_groups),
        in_specs=[pl.BlockSpec((seq, cg), lambda s, g: (r0 + s, g)),
                  pl.BlockSpec((1, cg, cg), lambda s, g: (g, 0, 0)),
                  pl.BlockSpec((1, cg), lambda s, g: (0, g))],
        out_specs=pl.BlockSpec((seq, cg), lambda s, g: (s, g)),
        out_shape=jax.ShapeDtypeStruct((n_seq * seq, n_groups * cg), BF16),
        scratch_shapes=[pltpu.VMEM((seq + 2 * POOL_HALO, cg), F32)],
        compiler_params=_cparams(("parallel", "arbitrary")), name="pool_mix",
    )(u, w_pool, pool_scale)


def _na_kernel(q_ref, k_ref, v_ref, kc_ref, vc_ref, t_ref, o_ref, bias_ref, *,
               rows_per_blk, key_rows, n_rows, wr, n_blk, scale):
    blk = pl.program_id(2)
    half = (key_rows - rows_per_blk) // 2
    W = GRID_W
    lane = lax.broadcasted_iota(jnp.int32, (W, 2 * W), 1)

    def build(b):
        k_start = min(max(b * rows_per_blk - half, 0), n_rows - key_rows)
        for qr in range(rows_per_blk):
            r = b * rows_per_blk + qr
            rs = min(max(r - wr // 2, 0), n_rows - wr)
            for kp in range(key_rows // 2):
                idx = [kk - r + NA_ROWS - 1 if rs <= kk < rs + wr else 2 * NA_ROWS - 1
                       for kk in (k_start + 2 * kp, k_start + 2 * kp + 1)]
                bias_ref[qr * W:(qr + 1) * W, kp * 2 * W:(kp + 1) * 2 * W] = jnp.where(
                    lane < W, t_ref[0, idx[0]], t_ref[0, idx[1]])

    pl.when(blk == 0)(functools.partial(build, 0))
    if n_blk > 2:
        pl.when(blk == 1)(functools.partial(build, 1))
    if n_blk > 1:
        pl.when(blk == n_blk - 1)(functools.partial(build, n_blk - 1))

    k_start = jnp.clip(blk * rows_per_blk - half, 0, n_rows - key_rows)
    tok0 = pl.multiple_of(k_start * W, W * 4)
    n_keys = key_rows * W
    q = q_ref[...].astype(BF16)
    kl = k_ref[pl.ds(tok0, n_keys), :].astype(BF16)
    vl = v_ref[pl.ds(tok0, n_keys), :].astype(BF16)
    kc = kc_ref[...].astype(BF16)
    vc = vc_ref[...].astype(BF16)
    dn = (((1,), (1,)), ((), ()))
    s_loc = lax.dot_general(q, kl, dn, preferred_element_type=F32) * scale + bias_ref[...]
    s_ctx = lax.dot_general(q, kc, dn, preferred_element_type=F32) * scale
    m = jnp.maximum(jnp.max(s_loc, axis=-1, keepdims=True), jnp.max(s_ctx, axis=-1, keepdims=True))
    p_loc = jnp.exp(s_loc - m)
    p_ctx = jnp.exp(s_ctx - m)
    l = jnp.sum(p_loc, axis=-1, keepdims=True) + jnp.sum(p_ctx, axis=-1, keepdims=True)
    o = (jnp.dot(p_ctx.astype(BF16), vc, preferred_element_type=F32)
         + jnp.dot(p_loc.astype(BF16), vl, preferred_element_type=F32)) / l
    o_ref[...] = o.astype(o_ref.dtype)


def _na_col_tables(rpb):
    H = rpb.shape[0]
    col = jnp.arange(GRID_W)
    cstart = jnp.clip(col - NA_COLS // 2, 0, GRID_W - NA_COLS)
    col_valid = (col[None, :] >= cstart[:, None]) & (col[None, :] < cstart[:, None] + NA_COLS)
    col_idx = jnp.clip(col[None, :] - col[:, None] + NA_COLS - 1, 0, 2 * NA_COLS - 2)
    onehot = (col_idx[None] == jnp.arange(2 * NA_COLS - 1)[:, None, None]).astype(F32)
    t = jnp.einsum("hrc,cqk->hrqk", rpb.astype(F32), onehot, precision=lax.Precision.HIGHEST)
    t = jnp.where(col_valid[None, None], t, NEG_INF)
    t = jnp.concatenate([t, jnp.full((H, 1, GRID_W, GRID_W), NEG_INF, F32)], axis=1)
    return jnp.concatenate([t, t], axis=-1)


def _na_attention(qkv, k_ctx, v_ctx, rpb, *, batch, n_tok, heads, row0, t_ctx):
    n_rows = n_tok // GRID_W
    wr = min(NA_ROWS, n_rows)
    rows_per_blk = min(NA_ROWS, n_rows)
    key_rows = min(rows_per_blk + wr, n_rows)
    n_blk = n_rows // rows_per_blk
    bq = rows_per_blk * GRID_W
    assert n_rows % rows_per_blk == 0 and row0 % n_tok == 0 and row0 % bq == 0 and key_rows % 2 == 0
    assert rows_per_blk % 4 == 0 and ((key_rows - rows_per_blk) // 2) % 4 == 0
    tables = _na_col_tables(rpb)
    qr0, kr0 = row0 // bq, row0 // n_tok
    return pl.pallas_call(
        functools.partial(_na_kernel, rows_per_blk=rows_per_blk, key_rows=key_rows, n_rows=n_rows, wr=wr,
                          n_blk=n_blk, scale=HEAD_DIM ** -0.5),
        grid=(heads, batch, n_blk),
        in_specs=[pl.BlockSpec((bq, HEAD_DIM), lambda h, b, i: (qr0 + b * n_blk + i, h)),
                  pl.BlockSpec((n_tok, HEAD_DIM), lambda h, b, i: (kr0 + b, heads + h)),
                  pl.BlockSpec((n_tok, HEAD_DIM), lambda h, b, i: (kr0 + b, 2 * heads + h)),
                  pl.BlockSpec((t_ctx, HEAD_DIM), lambda h, b, i: (b, h)),
                  pl.BlockSpec((t_ctx, HEAD_DIM), lambda h, b, i: (b, h)),
                  pl.BlockSpec((1, 2 * NA_ROWS, GRID_W, 2 * GRID_W), lambda h, b, i: (h, 0, 0, 0))],
        out_specs=pl.BlockSpec((bq, HEAD_DIM), lambda h, b, i: (b * n_blk + i, h)),
        out_shape=jax.ShapeDtypeStruct((batch * n_tok, heads * HEAD_DIM), BF16),
        scratch_shapes=[pltpu.VMEM((bq, key_rows * GRID_W), F32)],
        compiler_params=_cparams(("arbitrary", "arbitrary", "arbitrary")), name="na_attention",
    )(qkv, qkv, qkv, k_ctx, v_ctx, tables)


def _row_copy(src_hbm, row, dst, slot, sem):
    return pltpu.make_async_copy(src_hbm.at[pl.ds(row, 1), :], dst.at[pl.ds(slot, 1), :], sem)


def _gather_kernel(src_ref, x_hbm, o_ref, buf, sem, *, rt, n_steps):
    i = pl.program_id(0)
    cur = i % 2

    def issue(step, s):
        def body(r, c):
            _row_copy(x_hbm, src_ref[step * rt + r], buf.at[s], r, sem.at[s]).start()
            return c
        lax.fori_loop(0, rt, body, 0, unroll=8)

    @pl.when(i == 0)
    def _():
        issue(0, 0)

    @pl.when(i + 1 < n_steps)
    def _():
        issue(i + 1, 1 - cur)

    def wait(r, c):
        _row_copy(x_hbm, 0, buf.at[cur], r, sem.at[cur]).wait()
        return c
    lax.fori_loop(0, rt, wait, 0, unroll=8)
    o_ref[...] = buf[cur].astype(o_ref.dtype)


def _gather_rows(x, src, *, rt):
    n = src.shape[0]
    D = x.shape[1]
    return pl.pallas_call(
        functools.partial(_gather_kernel, rt=rt, n_steps=n // rt),
        out_shape=jax.ShapeDtypeStruct((n, D), BF16),
        grid_spec=pltpu.PrefetchScalarGridSpec(
            num_scalar_prefetch=1, grid=(n // rt,),
            in_specs=[pl.BlockSpec(memory_space=pl.ANY)],
            out_specs=pl.BlockSpec((rt, D), lambda i, s: (i, 0)),
            scratch_shapes=[pltpu.VMEM((2, rt, D), F32), pltpu.SemaphoreType.DMA((2,))]),
        compiler_params=_cparams(("arbitrary",)), name="moe_dispatch",
    )(src, x)


def _combine_kernel(pos_ref, x_ref, g_ref, y_hbm, o_ref, buf, sem, *, ct, n_steps):
    i = pl.program_id(0)
    cur = i % 2

    def issue(step, s):
        def body(r, c):
            for kk in range(TOP_K):
                _row_copy(y_hbm, pos_ref[TOP_K * (step * ct + r) + kk], buf.at[s, kk], r, sem.at[s]).start()
            return c
        lax.fori_loop(0, ct, body, 0, unroll=8)

    @pl.when(i == 0)
    def _():
        issue(0, 0)

    @pl.when(i + 1 < n_steps)
    def _():
        issue(i + 1, 1 - cur)

    def wait(r, c):
        for kk in range(TOP_K):
            _row_copy(y_hbm, 0, buf.at[cur, kk], r, sem.at[cur]).wait()
        return c
    lax.fori_loop(0, ct, wait, 0, unroll=8)
    acc = buf[cur, 0]
    for kk in range(1, TOP_K):
        acc = acc + buf[cur, kk]
    o_ref[...] = x_ref[...] + g_ref[0] * acc


def _combine(x, gate, y, pos, *, ct, gate_rows):
    T, D = x.shape
    return pl.pallas_call(
        functools.partial(_combine_kernel, ct=ct, n_steps=T // ct),
        out_shape=jax.ShapeDtypeStruct((T, D), F32),
        grid_spec=pltpu.PrefetchScalarGridSpec(
            num_scalar_prefetch=1, grid=(T // ct,),
            in_specs=[pl.BlockSpec((ct, D), lambda i, p: (i, 0)),
                      pl.BlockSpec((1, 1, D), lambda i, p: ((i * ct) // gate_rows, 0, 0)),
                      pl.BlockSpec(memory_space=pl.ANY)],
            out_specs=pl.BlockSpec((ct, D), lambda i, p: (i, 0)),
            scratch_shapes=[pltpu.VMEM((2, TOP_K, ct, D), F32), pltpu.SemaphoreType.DMA((2,))]),
        compiler_params=_cparams(("arbitrary",)), name="moe_combine",
    )(pos, x, gate, y)


def _routing_plan(top_i, top_w, n_experts, tm):
    T = top_i.shape[0]
    e_flat = top_i.reshape(-1)
    onehot = (e_flat[:, None] == jnp.arange(n_experts)[None, :]).astype(jnp.int32)
    counts = jnp.sum(onehot, axis=0)
    rank = jnp.sum((jnp.cumsum(onehot, axis=0) - onehot) * onehot, axis=1)
    gsz = ((counts + tm - 1) // tm) * tm
    gend = jnp.cumsum(gsz)
    goff = gend - gsz
    pos = (goff[e_flat] + rank).astype(jnp.int32)
    n_tiles = (T * TOP_K) // tm + n_experts
    src = jnp.zeros((n_tiles * tm,), jnp.int32).at[pos].set(jnp.arange(T * TOP_K, dtype=jnp.int32) // TOP_K)
    scale = jnp.zeros((n_tiles * tm,), F32).at[pos].set(top_w.reshape(-1))
    n_valid = (gend[-1] // tm).astype(jnp.int32)
    tile_row = jnp.minimum(jnp.arange(n_tiles, dtype=jnp.int32), n_valid - 1) * tm
    tile_expert = jnp.sum((tile_row[:, None] >= gend[None, :]).astype(jnp.int32), axis=1)
    return pos, src, scale[:, None], tile_expert.astype(jnp.int32), n_valid.reshape(1)


def _pick(n, prefs):
    for p in prefs:
        if n % p == 0:
            return p
    return n


def _modulation(cvecs, w, b):
    n_c, D = cvecs.shape
    rows = 8
    a = jnp.zeros((rows, D), BF16).at[:n_c].set(jax.nn.silu(cvecs).astype(BF16))
    m = _matmul(a, (w[None],), bm=rows, bn=_pick(6 * D, (1024, 512, 256, 128)), bk=D,
                mode="bias", extra=(b[None, :],), name="modulation")
    return jnp.transpose(m[:n_c].reshape(n_c, 6, 1, D), (1, 0, 2, 3))


def _rope_tables(n_prompt_rows, dec_batch, dec_seq):
    t = jnp.arange(dec_seq)
    pos = jnp.stack([t // GRID_W, t % GRID_W], axis=-1).astype(F32)
    n_freq = HEAD_DIM // 4
    inv = ROPE_THETA ** (-jnp.arange(n_freq, dtype=F32) / n_freq)
    ang = pos[:, :, None] * inv
    cos = jnp.concatenate([jnp.cos(ang)] * 2, axis=-1).reshape(dec_seq, HEAD_DIM)
    sin = jnp.concatenate([-jnp.sin(ang), jnp.sin(ang)], axis=-1).reshape(dec_seq, HEAD_DIM)
    cos = jnp.concatenate([jnp.ones((n_prompt_rows, HEAD_DIM), F32)] + [cos] * dec_batch)
    sin = jnp.concatenate([jnp.zeros((n_prompt_rows, HEAD_DIM), F32)] + [sin] * dec_batch)
    return cos, sin


def kernel(x_prompt, x_sample, c, c_ctx, cache_l0_attn_k, cache_l0_attn_v, cache_l1_na_k, cache_l1_na_v,
           l0_ada_w, l0_ada_b, l0_norm1, l0_w_in, l0_q_norm, l0_k_norm, l0_w_pool, l0_pool_scale, l0_w_out,
           l0_norm2, l0_ffn_w1, l0_ffn_w3, l0_ffn_w2,
           l1_ada_w, l1_ada_b, l1_norm1, l1_w_qkv, l1_rpb, l1_w_out, l1_norm2, l1_router,
           l1_exp_w1, l1_exp_w3, l1_exp_w2, final_norm):
    batch, seq, D = x_prompt.shape
    dec_batch, dec_seq, _ = x_sample.shape
    past_len = cache_l0_attn_k.shape[1]
    kv_heads0 = cache_l0_attn_k.shape[2]
    heads1 = cache_l1_na_k.shape[2]
    n_pool, cg, _ = l0_w_pool.shape
    pool_w = n_pool * cg
    kv_w = kv_heads0 * HEAD_DIM
    q_w = l0_w_in.shape[1] - pool_w - 2 * kv_w
    groups0 = q_w // kv_w
    c_w = heads1 * HEAD_DIM
    n_experts = l1_router.shape[1]
    Tp, Ts = batch * seq, dec_batch * dec_seq
    T = Tp + Ts
    group_rows = math.gcd(Tp, dec_seq)
    grp_to_c = jnp.array([0] * (Tp // group_rows) + [1 + b for b in range(dec_batch)
                                                      for _ in range(dec_seq // group_rows)], jnp.int32)

    bt = _pick(group_rows, (256, 128, 64, 32, 16, 8))
    bm = _pick(group_rows, (1024, 512, 256, 128, 64, 32, 16, 8))
    blk = lambda n: _pick(n, (512, 256, 128))
    seq_blk = lambda n: _pick(n, (256, 128, 64, 32, 16, 8))
    mods = lambda m: [m[i][grp_to_c] for i in range(6)]
    cvecs = jnp.concatenate([c_ctx[None, :], c], axis=0)
    gg = dict(gate_rows=group_rows)

    x = jnp.concatenate([x_prompt.reshape(Tp, D), x_sample.reshape(Ts, D)], axis=0)

    sh1, sc1, g1, sh2, sc2, g2 = mods(_modulation(cvecs, l0_ada_w, l0_ada_b))
    h = _adaln(x, l0_norm1[None], sc1, sh1, group_rows=group_rows, bt=bt, out_dtype=BF16)
    u = _matmul(h, (l0_w_in[None],), bm=bm, bn=blk(l0_w_in.shape[1]), bk=D, name="l0_in_proj")
    cos, sin = _rope_tables(Tp, dec_batch, dec_seq)
    q_att, k_att, k_state = _qk_prep(u, cos, sin, l0_q_norm[None], l0_k_norm[None],
                                     pool_w=pool_w, q_w=q_w, kv_w=kv_w, bt=bt)
    v_col0 = pool_w + q_w + kv_w
    att_p = _attention(q_att, k_att, u, batch=batch, s_len=seq, t_len=seq, kv_heads=kv_heads0, groups=groups0,
                       bq=seq_blk(seq), hp=kv_heads0, v_col0=v_col0)
    k_lat = jnp.concatenate([cache_l0_attn_k.reshape(dec_batch, past_len, kv_w).astype(BF16),
                             k_att[Tp:].reshape(dec_batch, dec_seq, kv_w)], axis=1).reshape(-1, kv_w)
    v_lat = jnp.concatenate([cache_l0_attn_v.reshape(dec_batch, past_len, kv_w),
                             u[Tp:, v_col0:].reshape(dec_batch, dec_seq, kv_w)], axis=1).reshape(-1, kv_w)
    att_s = _attention(q_att, k_lat, v_lat, batch=dec_batch, s_len=dec_seq, t_len=past_len + dec_seq,
                       kv_heads=kv_heads0, groups=groups0, bq=seq_blk(dec_seq), hp=1, q_row0=Tp)
    w_pool = l0_w_pool.astype(BF16)
    pool_p = _pool_mix(u, w_pool, l0_pool_scale[None], n_seq=batch, seq=seq, row0=0)
    pool_s = _pool_mix(u, w_pool, l0_pool_scale[None], n_seq=dec_batch, seq=dec_seq, row0=Tp)
    mix = jnp.concatenate([jnp.concatenate([pool_p, pool_s], axis=0),
                           jnp.concatenate([att_p, att_s], axis=0)], axis=1)
    x = _matmul(mix, (l0_w_out[None],), bm=bm, bn=blk(D), bk=mix.shape[1], mode="gated_res",
                extra=(x, g1), name="l0_out_proj", **gg)
    h = _adaln(x, l0_norm2[None], sc2, sh2, group_rows=group_rows, bt=bt, out_dtype=BF16)
    d_ff = l0_ffn_w1.shape[1]
    hh = _matmul(h, (l0_ffn_w1[None], l0_ffn_w3[None]), bm=bm, bn=blk(d_ff), bk=D,
                 mode="swiglu", out_dtype=BF16, name="l0_ffn_up")
    x = _matmul(hh, (l0_ffn_w2.astype(BF16)[None],), bm=bm, bn=blk(D),
                bk=_pick(d_ff, (5504, 2048, 1024, 512, 256, 128)),
                mode="gated_res", extra=(x, g2), name="l0_ffn_down", **gg)
    state_l0_k = k_state[:Tp].reshape(batch, seq, kv_heads0, HEAD_DIM)
    state_l0_v = u[:Tp, v_col0:].reshape(batch, seq, kv_heads0, HEAD_DIM)

    sh1, sc1, g1, sh2, sc2, g2 = mods(_modulation(cvecs, l1_ada_w, l1_ada_b))
    h = _adaln(x, l1_norm1[None], sc1, sh1, group_rows=group_rows, bt=bt, out_dtype=BF16)
    qkv = _matmul(h, (l1_w_qkv[None],), bm=bm, bn=blk(3 * c_w), bk=D, name="l1_qkv_proj")
    att_p = _attention(qkv, qkv, qkv, batch=batch, s_len=seq, t_len=seq, kv_heads=heads1, groups=1,
                       bq=seq_blk(seq), hp=_pick(heads1, (8, 4, 2, 1)), k_col0=c_w, v_col0=2 * c_w)
    att_s = _na_attention(qkv, cache_l1_na_k.reshape(dec_batch * past_len, c_w),
                          cache_l1_na_v.reshape(dec_batch * past_len, c_w), l1_rpb,
                          batch=dec_batch, n_tok=dec_seq, heads=heads1, row0=Tp, t_ctx=past_len)
    x = _matmul(jnp.concatenate([att_p, att_s], axis=0), (l1_w_out[None],), bm=bm, bn=blk(D), bk=c_w,
                mode="gated_res", extra=(x, g1), name="l1_out_proj", **gg)
    h32, top_i, top_w = _adaln(x, l1_norm2[None], sc2, sh2, group_rows=group_rows, bt=bt, out_dtype=F32,
                               router=l1_router)
    tm = _pick(T * TOP_K, (512, 256, 128, 64, 32, 16, 8))
    pos, src, row_scale, tile_expert, n_valid = _routing_plan(top_i[:, :TOP_K], top_w[:, :TOP_K], n_experts, tm)
    xs = _gather_rows(h32, src, rt=_pick(tm, (256, 128, 64, 32, 16, 8)))
    d_fe = l1_exp_w1.shape[2]
    he = _matmul(xs, (l1_exp_w1, l1_exp_w3), bm=tm, bn=blk(d_fe), bk=D, mode="swiglu",
                 out_dtype=BF16, tile_group=tile_expert, n_valid=n_valid, name="moe_up")
    ye = _matmul(he, (l1_exp_w2.astype(BF16),), bm=tm, bn=_pick(D, (1024, 512, 256, 128)),
                 bk=_pick(d_fe, (7168, 3584, 2048, 1024, 512, 256, 128)), mode="rowscale", extra=(row_scale,),
                 tile_group=tile_expert, n_valid=n_valid, name="moe_down")
    x = _combine(x, g2, ye, pos, ct=_pick(group_rows, (128, 64, 32, 16, 8)), gate_rows=group_rows)
    state_l1_k = qkv[:Tp, c_w:2 * c_w].reshape(batch, seq, heads1, HEAD_DIM)
    state_l1_v = qkv[:Tp, 2 * c_w:].reshape(batch, seq, heads1, HEAD_DIM)

    y_p = _rmsnorm(x, final_norm[None], bt=bt, row0=0, n_rows=Tp)
    y_s = _rmsnorm(x, final_norm[None], bt=bt, row0=Tp, n_rows=Ts)
    return (y_p.reshape(batch, seq, D), y_s.reshape(dec_batch, dec_seq, D),
            state_l0_k, state_l0_v, state_l1_k, state_l1_v)
```

```python
import functools
import math

import jax
import jax.numpy as jnp
from jax import lax
from jax.experimental import pallas as pl
from jax.experimental.pallas import tpu as pltpu

HEAD_DIM = 128
GRID_W = 64
POOL_WINDOWS = (2, 4, 8, 16)
NA_ROWS = 8
NA_COLS = 16
TOP_K = 2
EPS = 1e-6
ROPE_THETA = 10000.0
NEG_INF = -1e30
LOG2E = math.log2(math.e)
LANES = 128
POOL_HALO = 16
VMEM_LIMIT_BYTES = 60 * 1024 * 1024

F32 = jnp.float32
BF16 = jnp.bfloat16


def _cparams(sem):
    return pltpu.CompilerParams(dimension_semantics=sem, vmem_limit_bytes=VMEM_LIMIT_BYTES)


def _mm_kernel(grp_ref, nvalid_ref, *refs, nk, mode, cast_w):
    n_w = 2 if mode == "swiglu" else 1
    a_ref = refs[0]
    w_refs = refs[1:1 + n_w]
    n_extra = {"plain": 0, "swiglu": 0, "bias": 1, "rowscale": 1, "gated_res": 2}[mode]
    extra = refs[1 + n_w:1 + n_w + n_extra]
    o_ref = refs[1 + n_w + n_extra]
    scratch = refs[2 + n_w + n_extra:]
    m = pl.program_id(1)
    k = pl.program_id(2)

    def epilogue(accs):
        if mode == "plain":
            r = accs[0]
        elif mode == "swiglu":
            g = accs[0]
            r = (g / (1.0 + jnp.exp(-g))) * accs[1]
        elif mode == "bias":
            r = accs[0] + extra[0][...]
        elif mode == "rowscale":
            r = accs[0] * extra[0][...]
        else:
            r = extra[0][...] + extra[1][0] * accs[0]
        o_ref[...] = r.astype(o_ref.dtype)

    @pl.when(m < nvalid_ref[0])
    def _():
        if cast_w:
            @pl.when(jnp.logical_or(m == 0, grp_ref[m] != grp_ref[jnp.maximum(m - 1, 0)]))
            def _():
                for w_ref, wb in zip(w_refs, scratch):
                    wb[...] = w_ref[0].astype(BF16)
            ws = [wb[...] for wb in scratch]
        else:
            ws = [w_ref[0] for w_ref in w_refs]
        a = a_ref[...]
        parts = [jnp.dot(a, w, preferred_element_type=F32) for w in ws]
        if nk == 1:
            epilogue(parts)
        else:
            @pl.when(k == 0)
            def _():
                for acc, p in zip(scratch, parts):
                    acc[...] = p

            @pl.when(k > 0)
            def _():
                for acc, p in zip(scratch, parts):
                    acc[...] += p

            @pl.when(k == nk - 1)
            def _():
                epilogue([acc[...] for acc in scratch])

    @pl.when(jnp.logical_and(m >= nvalid_ref[0], k == nk - 1))
    def _():
        o_ref[...] = jnp.zeros(o_ref.shape, o_ref.dtype)


def _matmul(a, ws, *, bm, bn, bk, mode="plain", out_dtype=F32, extra=(), tile_group=None,
            n_valid=None, gate_rows=None, name="mm"):
    M, K = a.shape
    E, _, N = ws[0].shape
    nm, nn, nk = M // bm, N // bn, K // bk
    assert nm * bm == M and nn * bn == N and nk * bk == K, (a.shape, ws[0].shape, bm, bn, bk)
    cast_w = ws[0].dtype == F32
    assert not (cast_w and nk > 1)
    if tile_group is None:
        tile_group = jnp.zeros((nm,), jnp.int32)
        n_valid = jnp.full((1,), nm, jnp.int32)
    kk = lambda m, k: jnp.where(m % 2 == 1, nk - 1 - k, k) if nk > 1 else k
    in_specs = [pl.BlockSpec((bm, bk), lambda n, m, k, g, nv: (m, kk(m, k)))]
    for _ in ws:
        in_specs.append(pl.BlockSpec((1, bk, bn), lambda n, m, k, g, nv: (g[m], kk(m, k), n)))
    if mode == "bias":
        in_specs.append(pl.BlockSpec((1, bn), lambda n, m, k, g, nv: (0, n)))
    elif mode == "rowscale":
        in_specs.append(pl.BlockSpec((bm, 1), lambda n, m, k, g, nv: (m, 0)))
    elif mode == "gated_res":
        in_specs.append(pl.BlockSpec((bm, bn), lambda n, m, k, g, nv: (m, n)))
        in_specs.append(pl.BlockSpec((1, 1, bn), lambda n, m, k, g, nv: ((m * bm) // gate_rows, 0, n)))
    if cast_w:
        scratch = [pltpu.VMEM((bk, bn), BF16) for _ in ws]
    else:
        scratch = [] if nk == 1 else [pltpu.VMEM((bm, bn), F32) for _ in ws]
    return pl.pallas_call(
        functools.partial(_mm_kernel, nk=nk, mode=mode, cast_w=cast_w),
        out_shape=jax.ShapeDtypeStruct((M, N), out_dtype),
        grid_spec=pltpu.PrefetchScalarGridSpec(
            num_scalar_prefetch=2,
            grid=(nn, nm, nk),
            in_specs=in_specs,
            out_specs=pl.BlockSpec((bm, bn), lambda n, m, k, g, nv: (m, n)),
            scratch_shapes=scratch),
        compiler_params=_cparams(("arbitrary", "arbitrary", "arbitrary")),
        name=name,
    )(tile_group, n_valid, a, *ws, *extra)


def _moe_up_kernel(grp_ref, nvalid_ref, a_ref, w1a_ref, w3a_ref, w1b_ref, w3b_ref, o_ref,
                   wa_ref, wb_ref, par_ref, *, nm, kh):
    n = pl.program_id(0)
    s = pl.program_id(1)
    m = s - 1
    first = jnp.logical_and(n == 0, s == 0)
    g_cur = grp_ref[jnp.clip(m, 0, nm - 1)]
    g_prev = grp_ref[jnp.clip(m - 1, 0, nm - 1)]
    g_next = grp_ref[jnp.clip(m + 1, 0, nm - 1)]

    @pl.when(first)
    def _():
        par_ref[0] = 0

    @pl.when(jnp.logical_or(s == 1, jnp.logical_and(s > 1, g_cur != g_prev)))
    def _():
        par_ref[0] = 1 - par_ref[0]
        wb_ref[0] = w1b_ref[0].astype(BF16)
        wb_ref[1] = w3b_ref[0].astype(BF16)

    p = par_ref[0]

    nxt_new = jnp.logical_or(jnp.logical_or(first, s == nm),
                             jnp.logical_and(jnp.logical_and(s >= 1, s < nm), g_next != g_cur))

    @pl.when(nxt_new)
    def _():
        wa_ref[1 - p, 0] = w1a_ref[0].astype(BF16)
        wa_ref[1 - p, 1] = w3a_ref[0].astype(BF16)

    @pl.when(jnp.logical_and(s >= 1, m < nvalid_ref[0]))
    def _():
        a_lo = a_ref[:, :kh]
        a_hi = a_ref[:, kh:]
        g = (jnp.dot(a_lo, wa_ref[p, 0], preferred_element_type=F32)
             + jnp.dot(a_hi, wb_ref[0], preferred_element_type=F32))
        u = (jnp.dot(a_lo, wa_ref[p, 1], preferred_element_type=F32)
             + jnp.dot(a_hi, wb_ref[1], preferred_element_type=F32))
        o_ref[...] = ((g / (1.0 + jnp.exp(-g))) * u).astype(o_ref.dtype)

    @pl.when(jnp.logical_and(s >= 1, m >= nvalid_ref[0]))
    def _():
        o_ref[...] = jnp.zeros(o_ref.shape, o_ref.dtype)


def _moe_up(a, w1, w3, *, bm, bn, tile_group, n_valid):
    M, K = a.shape
    E, _, N = w1.shape
    nm, nn, kh = M // bm, N // bn, K // 2
    assert nm * bm == M and nn * bn == N and kh * 2 == K and kh % LANES == 0
    tile = lambda s: jnp.maximum(s - 1, 0)

    def w_ahead(n, s, g, nv):
        wrap = s == nm
        return g[jnp.where(wrap, 0, jnp.minimum(s, nm - 1))], 0, jnp.where(wrap, jnp.minimum(n + 1, nn - 1), n)

    w_now = lambda n, s, g, nv: (g[tile(s)], 1, n)
    return pl.pallas_call(
        functools.partial(_moe_up_kernel, nm=nm, kh=kh),
        out_shape=jax.ShapeDtypeStruct((M, N), BF16),
        grid_spec=pltpu.PrefetchScalarGridSpec(
            num_scalar_prefetch=2,
            grid=(nn, nm + 1),
            in_specs=[pl.BlockSpec((bm, K), lambda n, s, g, nv: (tile(s), 0)),
                      pl.BlockSpec((1, kh, bn), w_ahead), pl.BlockSpec((1, kh, bn), w_ahead),
                      pl.BlockSpec((1, kh, bn), w_now), pl.BlockSpec((1, kh, bn), w_now)],
            out_specs=pl.BlockSpec((bm, bn), lambda n, s, g, nv: (tile(s), n)),
            scratch_shapes=[pltpu.VMEM((2, 2, kh, bn), BF16), pltpu.VMEM((2, kh, bn), BF16),
                            pltpu.SMEM((1,), jnp.int32)]),
        compiler_params=_cparams(("arbitrary", "arbitrary")),
        name="moe_up",
    )(tile_group, n_valid, a, w1, w3, w1, w3)


def _rms(x, g):
    return x * lax.rsqrt(jnp.mean(x * x, axis=-1, keepdims=True) + EPS) * g


def _adaln_kernel(x_ref, g_ref, sc_ref, sh_ref, o_ref):
    y = _rms(x_ref[...], g_ref[...])
    o_ref[...] = (y * (1.0 + sc_ref[0]) + sh_ref[0]).astype(o_ref.dtype)


def _rmsnorm_kernel(x_ref, g_ref, o_ref):
    o_ref[...] = _rms(x_ref[...], g_ref[...]).astype(o_ref.dtype)


def _adaln_router_kernel(x_ref, g_ref, sc_ref, sh_ref, r_ref, o_ref, idx_ref, w_ref, *, n_experts):
    y = _rms(x_ref[...], g_ref[...])
    h = y * (1.0 + sc_ref[0]) + sh_ref[0]
    o_ref[...] = h
    logits = jnp.dot(h, r_ref[...], preferred_element_type=F32, precision=lax.Precision.HIGHEST)
    lane = lax.broadcasted_iota(jnp.int32, logits.shape, 1).astype(F32)
    lg = jnp.where(lane < n_experts, logits, -jnp.inf)
    v1 = jnp.max(lg, axis=-1, keepdims=True)
    i1 = jnp.min(jnp.where(lg == v1, lane, float(LANES)), axis=-1, keepdims=True)
    lg2 = jnp.where(lane == i1, -jnp.inf, lg)
    v2 = jnp.max(lg2, axis=-1, keepdims=True)
    i2 = jnp.min(jnp.where(lg2 == v2, lane, float(LANES)), axis=-1, keepdims=True)
    e = jnp.exp(v2 - v1)
    den = 1.0 + e
    idx_ref[...] = jnp.where(lane == 0, i1, jnp.where(lane == 1, i2, 0.0)).astype(jnp.int32)
    w_ref[...] = jnp.where(lane == 0, 1.0 / den, jnp.where(lane == 1, e / den, 0.0))


def _adaln(x, g, scale, shift, *, group_rows, bt, out_dtype, router=None):
    T, D = x.shape
    grid = (T // bt,)
    x_spec = pl.BlockSpec((bt, D), lambda i: (i, 0))
    g_spec = pl.BlockSpec((1, D), lambda i: (0, 0))
    mod_spec = pl.BlockSpec((1, 1, D), lambda i: ((i * bt) // group_rows, 0, 0))
    if router is None:
        return pl.pallas_call(
            _adaln_kernel, grid=grid, in_specs=[x_spec, g_spec, mod_spec, mod_spec], out_specs=x_spec,
            out_shape=jax.ShapeDtypeStruct((T, D), out_dtype),
            compiler_params=_cparams(("parallel",)), name="adaln",
        )(x, g, scale, shift)
    n_experts = router.shape[1]
    r_pad = jnp.pad(router, ((0, 0), (0, LANES - n_experts)))
    lane_spec = pl.BlockSpec((bt, LANES), lambda i: (i, 0))
    return pl.pallas_call(
        functools.partial(_adaln_router_kernel, n_experts=n_experts), grid=grid,
        in_specs=[x_spec, g_spec, mod_spec, mod_spec, pl.BlockSpec((D, LANES), lambda i: (0, 0))],
        out_specs=[x_spec, lane_spec, lane_spec],
        out_shape=[jax.ShapeDtypeStruct((T, D), F32), jax.ShapeDtypeStruct((T, LANES), jnp.int32),
                   jax.ShapeDtypeStruct((T, LANES), F32)],
        compiler_params=_cparams(("parallel",)), name="adaln_router",
    )(x, g, scale, shift, r_pad)


def _rmsnorm(x, g, *, bt, row0, n_rows):
    D = x.shape[1]
    r0 = row0 // bt
    assert r0 * bt == row0 and n_rows % bt == 0
    return pl.pallas_call(
        _rmsnorm_kernel, grid=(n_rows // bt,),
        in_specs=[pl.BlockSpec((bt, D), lambda i: (r0 + i, 0)), pl.BlockSpec((1, D), lambda i: (0, 0))],
        out_specs=pl.BlockSpec((bt, D), lambda i: (i, 0)), out_shape=jax.ShapeDtypeStruct((n_rows, D), F32),
        compiler_params=_cparams(("parallel",)), name="final_rmsnorm",
    )(x, g)


def _qk_prep_kernel(q0_ref, q1_ref, q2_ref, k_ref, cos_ref, sin_ref, qn_ref, kn_ref,
                    qo_ref, ko_ref, ks_ref):
    cos = cos_ref[...]
    sin = sin_ref[...]
    lane = lax.broadcasted_iota(jnp.int32, cos.shape, 1)
    first_half = (lane % (HEAD_DIM // 2)) < (HEAD_DIM // 4)

    def rope(y):
        partner = jnp.where(first_half, pltpu.roll(y, HEAD_DIM - HEAD_DIM // 4, 1),
                            pltpu.roll(y, HEAD_DIM // 4, 1))
        return y * cos + partner * sin

    q_heads_per_ref = q0_ref.shape[1] // HEAD_DIM
    for r, q_ref in enumerate((q0_ref, q1_ref, q2_ref)):
        for h in range(q_heads_per_ref):
            sl = slice(h * HEAD_DIM, (h + 1) * HEAD_DIM)
            y = _rms(q_ref[:, sl], qn_ref[...])
            col = (r * q_heads_per_ref + h) * HEAD_DIM
            qo_ref[:, col:col + HEAD_DIM] = rope(y).astype(qo_ref.dtype)
    for h in range(k_ref.shape[1] // HEAD_DIM):
        sl = slice(h * HEAD_DIM, (h + 1) * HEAD_DIM)
        y = _rms(k_ref[:, sl], kn_ref[...])
        ks_ref[:, sl] = y
        ko_ref[:, sl] = rope(y).astype(ko_ref.dtype)


def _qk_prep(u, cos, sin, q_norm, k_norm, *, pool_w, q_w, kv_w, bt):
    T = u.shape[0]
    assert q_w == 3 * kv_w and pool_w == kv_w
    cb = kv_w
    row = lambda i: (i, 0)
    in_specs = [pl.BlockSpec((bt, cb), lambda i, j=j: (i, j)) for j in (1, 2, 3, 4)]
    in_specs += [pl.BlockSpec((bt, HEAD_DIM), row), pl.BlockSpec((bt, HEAD_DIM), row),
                 pl.BlockSpec((1, HEAD_DIM), lambda i: (0, 0)), pl.BlockSpec((1, HEAD_DIM), lambda i: (0, 0))]
    return pl.pallas_call(
        _qk_prep_kernel, grid=(T // bt,), in_specs=in_specs,
        out_specs=[pl.BlockSpec((bt, q_w), row), pl.BlockSpec((bt, kv_w), row), pl.BlockSpec((bt, kv_w), row)],
        out_shape=[jax.ShapeDtypeStruct((T, q_w), BF16), jax.ShapeDtypeStruct((T, kv_w), BF16),
                   jax.ShapeDtypeStruct((T, kv_w), F32)],
        compiler_params=_cparams(("parallel",)), name="qk_norm_rope",
    )(u, u, u, u, cos, sin, q_norm, k_norm)


def _attn_kernel(q_ref, k_ref, v_ref, o_ref, *, heads, groups, scale):
    for h in range(heads):
        kv_sl = slice(h * HEAD_DIM, (h + 1) * HEAD_DIM)
        k = k_ref[:, kv_sl].astype(BF16)
        v = v_ref[:, kv_sl].astype(BF16)
        for g in range(groups):
            c = (h * groups + g) * HEAD_DIM
            q = q_ref[:, c:c + HEAD_DIM].astype(BF16)
            s = lax.dot_general(q, k, (((1,), (1,)), ((), ())), preferred_element_type=F32) * scale
            p = jnp.exp2(s - jnp.max(s, axis=-1, keepdims=True))
            l = jnp.sum(p, axis=-1, keepdims=True)
            o = jnp.dot(p.astype(BF16), v, preferred_element_type=F32) / l
            o_ref[:, c:c + HEAD_DIM] = o.astype(o_ref.dtype)


def _attention(q, k, v, *, batch, s_len, t_len, kv_heads, groups, bq, hp, q_row0=0, q_col0=0,
               k_row0=0, k_col0=0, v_col0=0):
    qw, kw = hp * groups * HEAD_DIM, hp * HEAD_DIM
    nq = s_len // bq
    assert q_row0 % bq == 0 and q_col0 % qw == 0 and k_row0 % t_len == 0 and kv_heads % hp == 0
    assert k_col0 % kw == 0 and v_col0 % kw == 0
    qr, qc, kr, kc, vc = q_row0 // bq, q_col0 // qw, k_row0 // t_len, k_col0 // kw, v_col0 // kw
    return pl.pallas_call(
        functools.partial(_attn_kernel, heads=hp, groups=groups, scale=HEAD_DIM ** -0.5 * LOG2E),
        grid=(batch, kv_heads // hp, nq),
        in_specs=[pl.BlockSpec((bq, qw), lambda b, h, i: (qr + b * nq + i, qc + h)),
                  pl.BlockSpec((t_len, kw), lambda b, h, i: (kr + b, kc + h)),
                  pl.BlockSpec((t_len, kw), lambda b, h, i: (kr + b, vc + h))],
        out_specs=pl.BlockSpec((bq, qw), lambda b, h, i: (b * nq + i, h)),
        out_shape=jax.ShapeDtypeStruct((batch * s_len, kv_heads * groups * HEAD_DIM), BF16),
        compiler_params=_cparams(("parallel", "parallel", "arbitrary")), name="attention",
    )(q, k, v)


def _pool_kernel(u_ref, w_ref, s_ref, o_ref, pad_ref, *, seq):
    g = pl.program_id(1)
    x = u_ref[...]
    pad_ref[0:POOL_HALO, :] = jnp.zeros((POOL_HALO, x.shape[1]), F32)
    pad_ref[POOL_HALO + seq:, :] = jnp.zeros((POOL_HALO, x.shape[1]), F32)
    pad_ref[POOL_HALO:POOL_HALO + seq, :] = x
    t = lax.broadcasted_iota(jnp.int32, x.shape, 0)
    for gi, win in enumerate(POOL_WINDOWS):
        @pl.when(g == gi)
        def _(win=win):
            back, fwd = win // 2, win - win // 2
            tot = pad_ref[POOL_HALO - back:POOL_HALO - back + seq, :]
            for j in range(-back + 1, fwd):
                tot = tot + pad_ref[POOL_HALO + j:POOL_HALO + j + seq, :]
            cnt = (jnp.minimum(t + fwd, seq) - jnp.maximum(t - back, 0)).astype(F32)
            diff = (tot / cnt - x).astype(BF16)
            y = jnp.dot(diff, w_ref[0], preferred_element_type=F32) * s_ref[...]
            o_ref[...] = y.astype(o_ref.dtype)


def _pool_mix(u, w_pool, pool_scale, *, n_seq, seq, row0):
    n_groups, cg, _ = w_pool.shape
    assert row0 % seq == 0 and max(POOL_WINDOWS) // 2 <= POOL_HALO and seq % 8 == 0
    r0 = row0 // seq
    return pl.pallas_call(
        functools.partial(_pool_kernel, seq=seq), grid=(n_seq, n_groups),
        in_specs=[pl.BlockSpec((seq, cg), lambda s, g: (r0 + s, g)),
                  pl.BlockSpec((1, cg, cg), lambda s, g: (g, 0, 0)),
                  pl.BlockSpec((1, cg), lambda s, g: (0, g))],
        out_specs=pl.BlockSpec((seq, cg), lambda s, g: (s, g)),
        out_shape=jax.ShapeDtypeStruct((n_seq * seq, n_groups * cg), BF16),
        scratch_shapes=[pltpu.VMEM((seq + 2 * POOL_HALO, cg), F32)],
        compiler_params=_cparams(("parallel", "arbitrary")), name="pool_mix",
    )(u, w_pool, pool_scale)


def _na_kernel(q_ref, k_ref, v_ref, kc_ref, vc_ref, t_ref, o_ref, bias_ref, *,
               rows_per_blk, key_rows, n_rows, wr, n_blk, scale):
    blk = pl.program_id(2)
    half = (key_rows - rows_per_blk) // 2
    W = GRID_W
    lane = lax.broadcasted_iota(jnp.int32, (W, 2 * W), 1)

    def build(b):
        k_start = min(max(b * rows_per_blk - half, 0), n_rows - key_rows)
        for qr in range(rows_per_blk):
            r = b * rows_per_blk + qr
            rs = min(max(r - wr // 2, 0), n_rows - wr)
            for kp in range(key_rows // 2):
                idx = [kk - r + NA_ROWS - 1 if rs <= kk < rs + wr else 2 * NA_ROWS - 1
                       for kk in (k_start + 2 * kp, k_start + 2 * kp + 1)]
                bias_ref[qr * W:(qr + 1) * W, kp * 2 * W:(kp + 1) * 2 * W] = jnp.where(
                    lane < W, t_ref[0, idx[0]], t_ref[0, idx[1]])

    pl.when(blk == 0)(functools.partial(build, 0))
    if n_blk > 2:
        pl.when(blk == 1)(functools.partial(build, 1))
    if n_blk > 1:
        pl.when(blk == n_blk - 1)(functools.partial(build, n_blk - 1))

    k_start = jnp.clip(blk * rows_per_blk - half, 0, n_rows - key_rows)
    tok0 = pl.multiple_of(k_start * W, W * 4)
    n_keys = key_rows * W
    q = q_ref[...].astype(BF16)
    kl = k_ref[pl.ds(tok0, n_keys), :].astype(BF16)
    vl = v_ref[pl.ds(tok0, n_keys), :].astype(BF16)
    kc = kc_ref[...].astype(BF16)
    vc = vc_ref[...].astype(BF16)
    dn = (((1,), (1,)), ((), ()))
    s_loc = lax.dot_general(q, kl, dn, preferred_element_type=F32) * scale + bias_ref[...]
    s_ctx = lax.dot_general(q, kc, dn, preferred_element_type=F32) * scale
    m = jnp.maximum(jnp.max(s_loc, axis=-1, keepdims=True), jnp.max(s_ctx, axis=-1, keepdims=True))
    p_loc = jnp.exp2(s_loc - m)
    p_ctx = jnp.exp2(s_ctx - m)
    l = jnp.sum(p_loc, axis=-1, keepdims=True) + jnp.sum(p_ctx, axis=-1, keepdims=True)
    o = (jnp.dot(p_ctx.astype(BF16), vc, preferred_element_type=F32)
         + jnp.dot(p_loc.astype(BF16), vl, preferred_element_type=F32)) / l
    o_ref[...] = o.astype(o_ref.dtype)


def _na_col_tables(rpb):
    H = rpb.shape[0]
    col = jnp.arange(GRID_W)
    cstart = jnp.clip(col - NA_COLS // 2, 0, GRID_W - NA_COLS)
    col_valid = (col[None, :] >= cstart[:, None]) & (col[None, :] < cstart[:, None] + NA_COLS)
    col_idx = jnp.clip(col[None, :] - col[:, None] + NA_COLS - 1, 0, 2 * NA_COLS - 2)
    onehot = (col_idx[None] == jnp.arange(2 * NA_COLS - 1)[:, None, None]).astype(F32)
    t = jnp.einsum("hrc,cqk->hrqk", rpb.astype(F32), onehot, precision=lax.Precision.HIGHEST)
    t = jnp.where(col_valid[None, None], t * LOG2E, NEG_INF)
    t = jnp.concatenate([t, jnp.full((H, 1, GRID_W, GRID_W), NEG_INF, F32)], axis=1)
    return jnp.concatenate([t, t], axis=-1)


def _na_attention(qkv, k_ctx, v_ctx, rpb, *, batch, n_tok, heads, row0, t_ctx):
    n_rows = n_tok // GRID_W
    wr = min(NA_ROWS, n_rows)
    rows_per_blk = min(NA_ROWS, n_rows)
    key_rows = min(rows_per_blk + wr, n_rows)
    n_blk = n_rows // rows_per_blk
    bq = rows_per_blk * GRID_W
    assert n_rows % rows_per_blk == 0 and row0 % n_tok == 0 and row0 % bq == 0 and key_rows % 2 == 0
    assert rows_per_blk % 4 == 0 and ((key_rows - rows_per_blk) // 2) % 4 == 0
    tables = _na_col_tables(rpb)
    qr0, kr0 = row0 // bq, row0 // n_tok
    return pl.pallas_call(
        functools.partial(_na_kernel, rows_per_blk=rows_per_blk, key_rows=key_rows, n_rows=n_rows, wr=wr,
                          n_blk=n_blk, scale=HEAD_DIM ** -0.5 * LOG2E),
        grid=(heads, batch, n_blk),
        in_specs=[pl.BlockSpec((bq, HEAD_DIM), lambda h, b, i: (qr0 + b * n_blk + i, h)),
                  pl.BlockSpec((n_tok, HEAD_DIM), lambda h, b, i: (kr0 + b, heads + h)),
                  pl.BlockSpec((n_tok, HEAD_DIM), lambda h, b, i: (kr0 + b, 2 * heads + h)),
                  pl.BlockSpec((t_ctx, HEAD_DIM), lambda h, b, i: (b, h)),
                  pl.BlockSpec((t_ctx, HEAD_DIM), lambda h, b, i: (b, h)),
                  pl.BlockSpec((1, 2 * NA_ROWS, GRID_W, 2 * GRID_W), lambda h, b, i: (h, 0, 0, 0))],
        out_specs=pl.BlockSpec((bq, HEAD_DIM), lambda h, b, i: (b * n_blk + i, h)),
        out_shape=jax.ShapeDtypeStruct((batch * n_tok, heads * HEAD_DIM), BF16),
        scratch_shapes=[pltpu.VMEM((bq, key_rows * GRID_W), F32)],
        compiler_params=_cparams(("arbitrary", "arbitrary", "arbitrary")), name="na_attention",
    )(qkv, qkv, qkv, k_ctx, v_ctx, tables)


def _row_copy(src_hbm, row, dst, slot, sem):
    return pltpu.make_async_copy(src_hbm.at[pl.ds(row, 1), :], dst.at[pl.ds(slot, 1), :], sem)


def _gather_kernel(src_ref, x_hbm, o_ref, buf, sem, *, rt, n_steps):
    i = pl.program_id(0)
    cur = i % 2

    def issue(step, s):
        def body(r, c):
            _row_copy(x_hbm, src_ref[step * rt + r], buf.at[s], r, sem.at[s]).start()
            return c
        lax.fori_loop(0, rt, body, 0, unroll=8)

    @pl.when(i == 0)
    def _():
        issue(0, 0)

    @pl.when(i + 1 < n_steps)
    def _():
        issue(i + 1, 1 - cur)

    def wait(r, c):
        _row_copy(x_hbm, 0, buf.at[cur], r, sem.at[cur]).wait()
        return c
    lax.fori_loop(0, rt, wait, 0, unroll=8)
    o_ref[...] = buf[cur].astype(o_ref.dtype)


def _gather_rows(x, src, *, rt):
    n = src.shape[0]
    D = x.shape[1]
    return pl.pallas_call(
        functools.partial(_gather_kernel, rt=rt, n_steps=n // rt),
        out_shape=jax.ShapeDtypeStruct((n, D), BF16),
        grid_spec=pltpu.PrefetchScalarGridSpec(
            num_scalar_prefetch=1, grid=(n // rt,),
            in_specs=[pl.BlockSpec(memory_space=pl.ANY)],
            out_specs=pl.BlockSpec((rt, D), lambda i, s: (i, 0)),
            scratch_shapes=[pltpu.VMEM((2, rt, D), F32), pltpu.SemaphoreType.DMA((2,))]),
        compiler_params=_cparams(("arbitrary",)), name="moe_dispatch",
    )(src, x)


def _combine_kernel(pos_ref, x_ref, g_ref, y_hbm, o_ref, buf, sem, *, ct, n_steps):
    i = pl.program_id(0)
    cur = i % 2

    def issue(step, s):
        def body(r, c):
            for kk in range(TOP_K):
                _row_copy(y_hbm, pos_ref[TOP_K * (step * ct + r) + kk], buf.at[s, kk], r, sem.at[s]).start()
            return c
        lax.fori_loop(0, ct, body, 0, unroll=8)

    @pl.when(i == 0)
    def _():
        issue(0, 0)

    @pl.when(i + 1 < n_steps)
    def _():
        issue(i + 1, 1 - cur)

    def wait(r, c):
        for kk in range(TOP_K):
            _row_copy(y_hbm, 0, buf.at[cur, kk], r, sem.at[cur]).wait()
        return c
    lax.fori_loop(0, ct, wait, 0, unroll=8)
    acc = buf[cur, 0]
    for kk in range(1, TOP_K):
        acc = acc + buf[cur, kk]
    o_ref[...] = x_ref[...] + g_ref[0] * acc


def _combine(x, gate, y, pos, *, ct, gate_rows):
    T, D = x.shape
    return pl.pallas_call(
        functools.partial(_combine_kernel, ct=ct, n_steps=T // ct),
        out_shape=jax.ShapeDtypeStruct((T, D), F32),
        grid_spec=pltpu.PrefetchScalarGridSpec(
            num_scalar_prefetch=1, grid=(T // ct,),
            in_specs=[pl.BlockSpec((ct, D), lambda i, p: (i, 0)),
                      pl.BlockSpec((1, 1, D), lambda i, p: ((i * ct) // gate_rows, 0, 0)),
                      pl.BlockSpec(memory_space=pl.ANY)],
            out_specs=pl.BlockSpec((ct, D), lambda i, p: (i, 0)),
            scratch_shapes=[pltpu.VMEM((2, TOP_K, ct, D), F32), pltpu.SemaphoreType.DMA((2,))]),
        compiler_params=_cparams(("arbitrary",)), name="moe_combine",
    )(pos, x, gate, y)


def _routing_plan(top_i, top_w, n_experts, tm):
    T = top_i.shape[0]
    e_flat = top_i.reshape(-1)
    onehot = (e_flat[:, None] == jnp.arange(n_experts)[None, :]).astype(jnp.int32)
    counts = jnp.sum(onehot, axis=0)
    rank = jnp.sum((jnp.cumsum(onehot, axis=0) - onehot) * onehot, axis=1)
    gsz = ((counts + tm - 1) // tm) * tm
    gend = jnp.cumsum(gsz)
    goff = gend - gsz
    pos = (goff[e_flat] + rank).astype(jnp.int32)
    n_tiles = (T * TOP_K) // tm + n_experts
    src = jnp.zeros((n_tiles * tm,), jnp.int32).at[pos].set(jnp.arange(T * TOP_K, dtype=jnp.int32) // TOP_K)
    scale = jnp.zeros((n_tiles * tm,), F32).at[pos].set(top_w.reshape(-1))
    n_valid = (gend[-1] // tm).astype(jnp.int32)
    tile_row = jnp.minimum(jnp.arange(n_tiles, dtype=jnp.int32), n_valid - 1) * tm
    tile_expert = jnp.sum((tile_row[:, None] >= gend[None, :]).astype(jnp.int32), axis=1)
    return pos, src, scale[:, None], tile_expert.astype(jnp.int32), n_valid.reshape(1)


def _pick(n, prefs):
    for p in prefs:
        if n % p == 0:
            return p
    return n


def _modulation(cvecs, w, b):
    n_c, D = cvecs.shape
    rows = 8
    a = jnp.zeros((rows, D), BF16).at[:n_c].set(jax.nn.silu(cvecs).astype(BF16))
    m = _matmul(a, (w[None],), bm=rows, bn=_pick(6 * D, (1024, 512, 256, 128)), bk=D,
                mode="bias", extra=(b[None, :],), name="modulation")
    return jnp.transpose(m[:n_c].reshape(n_c, 6, 1, D), (1, 0, 2, 3))


def _rope_tables(n_prompt_rows, dec_batch, dec_seq):
    t = jnp.arange(dec_seq)
    pos = jnp.stack([t // GRID_W, t % GRID_W], axis=-1).astype(F32)
    n_freq = HEAD_DIM // 4
    inv = ROPE_THETA ** (-jnp.arange(n_freq, dtype=F32) / n_freq)
    ang = pos[:, :, None] * inv
    cos = jnp.concatenate([jnp.cos(ang)] * 2, axis=-1).reshape(dec_seq, HEAD_DIM)
    sin = jnp.concatenate([-jnp.sin(ang), jnp.sin(ang)], axis=-1).reshape(dec_seq, HEAD_DIM)
    cos = jnp.concatenate([jnp.ones((n_prompt_rows, HEAD_DIM), F32)] + [cos] * dec_batch)
    sin = jnp.concatenate([jnp.zeros((n_prompt_rows, HEAD_DIM), F32)] + [sin] * dec_batch)
    return cos, sin


def kernel(x_prompt, x_sample, c, c_ctx, cache_l0_attn_k, cache_l0_attn_v, cache_l1_na_k, cache_l1_na_v,
           l0_ada_w, l0_ada_b, l0_norm1, l0_w_in, l0_q_norm, l0_k_norm, l0_w_pool, l0_pool_scale, l0_w_out,
           l0_norm2, l0_ffn_w1, l0_ffn_w3, l0_ffn_w2,
           l1_ada_w, l1_ada_b, l1_norm1, l1_w_qkv, l1_rpb, l1_w_out, l1_norm2, l1_router,
           l1_exp_w1, l1_exp_w3, l1_exp_w2, final_norm):
    batch, seq, D = x_prompt.shape
    dec_batch, dec_seq, _ = x_sample.shape
    past_len = cache_l0_attn_k.shape[1]
    kv_heads0 = cache_l0_attn_k.shape[2]
    heads1 = cache_l1_na_k.shape[2]
    n_pool, cg, _ = l0_w_pool.shape
    pool_w = n_pool * cg
    kv_w = kv_heads0 * HEAD_DIM
    q_w = l0_w_in.shape[1] - pool_w - 2 * kv_w
    groups0 = q_w // kv_w
    c_w = heads1 * HEAD_DIM
    n_experts = l1_router.shape[1]
    Tp, Ts = batch * seq, dec_batch * dec_seq
    T = Tp + Ts
    group_rows = math.gcd(Tp, dec_seq)
    grp_to_c = jnp.array([0] * (Tp // group_rows) + [1 + b for b in range(dec_batch)
                                                      for _ in range(dec_seq // group_rows)], jnp.int32)

    bt = _pick(group_rows, (256, 128, 64, 32, 16, 8))
    bm = _pick(group_rows, (1024, 512, 256, 128, 64, 32, 16, 8))
    blk = lambda n: _pick(n, (512, 256, 128))
    seq_blk = lambda n: _pick(n, (256, 128, 64, 32, 16, 8))
    mods = lambda m: [m[i][grp_to_c] for i in range(6)]
    cvecs = jnp.concatenate([c_ctx[None, :], c], axis=0)
    gg = dict(gate_rows=group_rows)

    x = jnp.concatenate([x_prompt.reshape(Tp, D), x_sample.reshape(Ts, D)], axis=0)

    sh1, sc1, g1, sh2, sc2, g2 = mods(_modulation(cvecs, l0_ada_w, l0_ada_b))
    h = _adaln(x, l0_norm1[None], sc1, sh1, group_rows=group_rows, bt=bt, out_dtype=BF16)
    u = _matmul(h, (l0_w_in[None],), bm=bm, bn=blk(l0_w_in.shape[1]), bk=D, name="l0_in_proj")
    cos, sin = _rope_tables(Tp, dec_batch, dec_seq)
    q_att, k_att, k_state = _qk_prep(u, cos, sin, l0_q_norm[None], l0_k_norm[None],
                                     pool_w=pool_w, q_w=q_w, kv_w=kv_w, bt=bt)
    v_col0 = pool_w + q_w + kv_w
    att_p = _attention(q_att, k_att, u, batch=batch, s_len=seq, t_len=seq, kv_heads=kv_heads0, groups=groups0,
                       bq=seq_blk(seq), hp=kv_heads0, v_col0=v_col0)
    k_lat = jnp.concatenate([cache_l0_attn_k.reshape(dec_batch, past_len, kv_w).astype(BF16),
                             k_att[Tp:].reshape(dec_batch, dec_seq, kv_w)], axis=1).reshape(-1, kv_w)
    v_lat = jnp.concatenate([cache_l0_attn_v.reshape(dec_batch, past_len, kv_w),
                             u[Tp:, v_col0:].reshape(dec_batch, dec_seq, kv_w)], axis=1).reshape(-1, kv_w)
    att_s = _attention(q_att, k_lat, v_lat, batch=dec_batch, s_len=dec_seq, t_len=past_len + dec_seq,
                       kv_heads=kv_heads0, groups=groups0, bq=seq_blk(dec_seq), hp=1, q_row0=Tp)
    w_pool = l0_w_pool.astype(BF16)
    pool_p = _pool_mix(u, w_pool, l0_pool_scale[None], n_seq=batch, seq=seq, row0=0)
    pool_s = _pool_mix(u, w_pool, l0_pool_scale[None], n_seq=dec_batch, seq=dec_seq, row0=Tp)
    mix = jnp.concatenate([jnp.concatenate([pool_p, pool_s], axis=0),
                           jnp.concatenate([att_p, att_s], axis=0)], axis=1)
    x = _matmul(mix, (l0_w_out[None],), bm=bm, bn=blk(D), bk=mix.shape[1], mode="gated_res",
                extra=(x, g1), name="l0_out_proj", **gg)
    h = _adaln(x, l0_norm2[None], sc2, sh2, group_rows=group_rows, bt=bt, out_dtype=BF16)
    d_ff = l0_ffn_w1.shape[1]
    hh = _matmul(h, (l0_ffn_w1[None], l0_ffn_w3[None]), bm=bm, bn=blk(d_ff), bk=D,
                 mode="swiglu", out_dtype=BF16, name="l0_ffn_up")
    x = _matmul(hh, (l0_ffn_w2.astype(BF16)[None],), bm=bm, bn=blk(D),
                bk=_pick(d_ff, (5504, 2048, 1024, 512, 256, 128)),
                mode="gated_res", extra=(x, g2), name="l0_ffn_down", **gg)
    state_l0_k = k_state[:Tp].reshape(batch, seq, kv_heads0, HEAD_DIM)
    state_l0_v = u[:Tp, v_col0:].reshape(batch, seq, kv_heads0, HEAD_DIM)

    sh1, sc1, g1, sh2, sc2, g2 = mods(_modulation(cvecs, l1_ada_w, l1_ada_b))
    h = _adaln(x, l1_norm1[None], sc1, sh1, group_rows=group_rows, bt=bt, out_dtype=BF16)
    qkv = _matmul(h, (l1_w_qkv[None],), bm=bm, bn=blk(3 * c_w), bk=D, name="l1_qkv_proj")
    att_p = _attention(qkv, qkv, qkv, batch=batch, s_len=seq, t_len=seq, kv_heads=heads1, groups=1,
                       bq=seq_blk(seq), hp=_pick(heads1, (8, 4, 2, 1)), k_col0=c_w, v_col0=2 * c_w)
    att_s = _na_attention(qkv, cache_l1_na_k.reshape(dec_batch * past_len, c_w),
                          cache_l1_na_v.reshape(dec_batch * past_len, c_w), l1_rpb,
                          batch=dec_batch, n_tok=dec_seq, heads=heads1, row0=Tp, t_ctx=past_len)
    x = _matmul(jnp.concatenate([att_p, att_s], axis=0), (l1_w_out[None],), bm=bm, bn=blk(D), bk=c_w,
                mode="gated_res", extra=(x, g1), name="l1_out_proj", **gg)
    h32, top_i, top_w = _adaln(x, l1_norm2[None], sc2, sh2, group_rows=group_rows, bt=bt, out_dtype=F32,
                               router=l1_router)
    tm = _pick(T * TOP_K, (512, 256, 128, 64, 32, 16, 8))
    pos, src, row_scale, tile_expert, n_valid = _routing_plan(top_i[:, :TOP_K], top_w[:, :TOP_K], n_experts, tm)
    xs = _gather_rows(h32, src, rt=_pick(tm, (256, 128, 64, 32, 16, 8)))
    d_fe = l1_exp_w1.shape[2]
    he = _moe_up(xs, l1_exp_w1, l1_exp_w3, bm=tm, bn=blk(d_fe), tile_group=tile_expert, n_valid=n_valid)
    ye = _matmul(he, (l1_exp_w2.astype(BF16),), bm=tm, bn=_pick(D, (1024, 512, 256, 128)),
                 bk=_pick(d_fe, (7168, 3584, 2048, 1024, 512, 256, 128)), mode="rowscale", extra=(row_scale,),
                 tile_group=tile_expert, n_valid=n_valid, name="moe_down")
    x = _combine(x, g2, ye, pos, ct=_pick(group_rows, (128, 64, 32, 16, 8)), gate_rows=group_rows)
    state_l1_k = qkv[:Tp, c_w:2 * c_w].reshape(batch, seq, heads1, HEAD_DIM)
    state_l1_v = qkv[:Tp, 2 * c_w:].reshape(batch, seq, heads1, HEAD_DIM)

    y_p = _rmsnorm(x, final_norm[None], bt=bt, row0=0, n_rows=Tp)
    y_s = _rmsnorm(x, final_norm[None], bt=bt, row0=Tp, n_rows=Ts)
    return (y_p.reshape(batch, seq, D), y_s.reshape(dec_batch, dec_seq, D),
            state_l0_k, state_l0_v, state_l1_k, state_l1_v)
```

```python
import functools
import math

import jax
import jax.numpy as jnp
from jax import lax
from jax.experimental import pallas as pl
from jax.experimental.pallas import tpu as pltpu

HEAD_DIM = 128
GRID_W = 64
POOL_WINDOWS = (2, 4, 8, 16)
NA_ROWS = 8
NA_COLS = 16
TOP_K = 2
EPS = 1e-6
ROPE_THETA = 10000.0
NEG_INF = -1e30
LOG2E = math.log2(math.e)
LANES = 128
POOL_HALO = 16
VMEM_LIMIT_BYTES = 60 * 1024 * 1024

F32 = jnp.float32
BF16 = jnp.bfloat16


def _cparams(sem):
    return pltpu.CompilerParams(dimension_semantics=sem, vmem_limit_bytes=VMEM_LIMIT_BYTES)


def _partial_tile(rows, bm, sub, k_last, compute, o_ref):
    nearly_full = rows > bm - sub

    @pl.when(nearly_full)
    def _():
        compute(slice(0, bm))

    @pl.when(jnp.logical_not(nearly_full))
    def _():
        for j in range(bm // sub):
            c = slice(j * sub, (j + 1) * sub)
            if j < bm // sub - 1:
                pl.when(rows > j * sub)(functools.partial(compute, c))

            @pl.when(jnp.logical_and(rows <= j * sub, k_last))
            def _(c=c):
                o_ref[c, :] = jnp.zeros((sub, o_ref.shape[1]), o_ref.dtype)


def _mm_kernel(grp_ref, nvalid_ref, rows_ref, *refs, nk, mode, cast_w, sub, n_first):
    n_w = 2 if mode == "swiglu" else 1
    a_ref = refs[0]
    w_refs = refs[1:1 + n_w]
    n_extra = {"plain": 0, "swiglu": 0, "bias": 1, "gated_res": 2 if n_first is None else 3}[mode]
    extra = refs[1 + n_w:1 + n_w + n_extra]
    o_ref = refs[1 + n_w + n_extra]
    scratch = refs[2 + n_w + n_extra:]
    bm = a_ref.shape[0]
    m = pl.program_id(1)
    k = pl.program_id(2)

    def epilogue(accs, rs):
        if mode == "plain":
            r = accs[0]
        elif mode == "swiglu":
            g = accs[0]
            r = (g / (1.0 + jnp.exp(-g))) * accs[1]
        elif mode == "bias":
            r = accs[0] + extra[0][...]
        else:
            res = extra[0][rs, :] if n_first is None else jnp.where(m < n_first, extra[0][rs, :], extra[1][rs, :])
            r = res + extra[-1][0] * accs[0]
        o_ref[rs, :] = r.astype(o_ref.dtype)

    def compute(rs):
        ws = [wb[...] for wb in scratch] if cast_w else [w_ref[0] for w_ref in w_refs]
        a = a_ref[rs, :]
        parts = [jnp.dot(a, w, preferred_element_type=F32) for w in ws]
        if nk == 1:
            epilogue(parts, rs)
        else:
            @pl.when(k == 0)
            def _():
                for acc, p in zip(scratch, parts):
                    acc[rs, :] = p

            @pl.when(k > 0)
            def _():
                for acc, p in zip(scratch, parts):
                    acc[rs, :] += p

            @pl.when(k == nk - 1)
            def _():
                epilogue([acc[rs, :] for acc in scratch], rs)

    @pl.when(m < nvalid_ref[0])
    def _():
        if cast_w:
            @pl.when(jnp.logical_or(m == 0, grp_ref[m] != grp_ref[jnp.maximum(m - 1, 0)]))
            def _():
                for w_ref, wb in zip(w_refs, scratch):
                    wb[...] = w_ref[0].astype(BF16)
        if sub is None:
            compute(slice(0, bm))
        else:
            _partial_tile(rows_ref[m], bm, sub, k == nk - 1, compute, o_ref)

    @pl.when(jnp.logical_and(m >= nvalid_ref[0], k == nk - 1))
    def _():
        o_ref[...] = jnp.zeros(o_ref.shape, o_ref.dtype)


def _matmul(a, ws, *, bm, bn, bk, mode="plain", out_dtype=F32, extra=(), tile_group=None,
            n_valid=None, tile_rows=None, sub=None, gate_rows=None, n_first=None, name="mm"):
    M, K = a.shape
    E, _, N = ws[0].shape
    nm, nn, nk = M // bm, N // bn, K // bk
    assert nm * bm == M and nn * bn == N and nk * bk == K, (a.shape, ws[0].shape, bm, bn, bk)
    cast_w = ws[0].dtype == F32
    assert not (cast_w and nk > 1)
    if tile_group is None:
        tile_group = jnp.zeros((nm,), jnp.int32)
        n_valid = jnp.full((1,), nm, jnp.int32)
    if tile_rows is None:
        tile_rows = jnp.full((nm,), bm, jnp.int32)
    kk = lambda m, k: jnp.where(m % 2 == 1, nk - 1 - k, k) if nk > 1 else k
    in_specs = [pl.BlockSpec((bm, bk), lambda n, m, k, *_: (m, kk(m, k)))]
    for _ in ws:
        in_specs.append(pl.BlockSpec((1, bk, bn), lambda n, m, k, g, *_: (g[m], kk(m, k), n)))
    if mode == "bias":
        in_specs.append(pl.BlockSpec((1, bn), lambda n, m, k, *_: (0, n)))
    elif mode == "gated_res":
        if n_first is None:
            in_specs.append(pl.BlockSpec((bm, bn), lambda n, m, k, *_: (m, n)))
        else:
            in_specs.append(pl.BlockSpec((bm, bn), lambda n, m, k, *_: (jnp.minimum(m, n_first - 1), n)))
            in_specs.append(pl.BlockSpec((bm, bn), lambda n, m, k, *_: (jnp.maximum(m - n_first, 0), n)))
        in_specs.append(pl.BlockSpec((1, 1, bn), lambda n, m, k, *_: ((m * bm) // gate_rows, 0, n)))
    if cast_w:
        scratch = [pltpu.VMEM((bk, bn), BF16) for _ in ws]
    else:
        scratch = [] if nk == 1 else [pltpu.VMEM((bm, bn), F32) for _ in ws]
    return pl.pallas_call(
        functools.partial(_mm_kernel, nk=nk, mode=mode, cast_w=cast_w, sub=sub, n_first=n_first),
        out_shape=jax.ShapeDtypeStruct((M, N), out_dtype),
        grid_spec=pltpu.PrefetchScalarGridSpec(
            num_scalar_prefetch=3,
            grid=(nn, nm, nk),
            in_specs=in_specs,
            out_specs=pl.BlockSpec((bm, bn), lambda n, m, k, *_: (m, n)),
            scratch_shapes=scratch),
        compiler_params=_cparams(("arbitrary", "arbitrary", "arbitrary")),
        name=name,
    )(tile_group, n_valid, tile_rows, a, *ws, *extra)


def _moe_up_kernel(grp_ref, nvalid_ref, rows_ref, a_ref, w1a_ref, w3a_ref, w1b_ref, w3b_ref, o_ref,
                   wa_ref, wb_ref, par_ref, *, nm, kh, sub):
    n = pl.program_id(0)
    s = pl.program_id(1)
    m = s - 1
    first = jnp.logical_and(n == 0, s == 0)
    g_cur = grp_ref[jnp.clip(m, 0, nm - 1)]
    g_prev = grp_ref[jnp.clip(m - 1, 0, nm - 1)]
    g_next = grp_ref[jnp.clip(m + 1, 0, nm - 1)]

    @pl.when(first)
    def _():
        par_ref[0] = 0

    @pl.when(jnp.logical_or(s == 1, jnp.logical_and(s > 1, g_cur != g_prev)))
    def _():
        par_ref[0] = 1 - par_ref[0]
        wb_ref[0] = w1b_ref[0].astype(BF16)
        wb_ref[1] = w3b_ref[0].astype(BF16)

    p = par_ref[0]

    nxt_new = jnp.logical_or(jnp.logical_or(first, s == nm),
                             jnp.logical_and(jnp.logical_and(s >= 1, s < nm), g_next != g_cur))

    @pl.when(nxt_new)
    def _():
        wa_ref[1 - p, 0] = w1a_ref[0].astype(BF16)
        wa_ref[1 - p, 1] = w3a_ref[0].astype(BF16)

    def compute(rs):
        a_lo = a_ref[rs, :kh]
        a_hi = a_ref[rs, kh:]
        g = (jnp.dot(a_lo, wa_ref[p, 0], preferred_element_type=F32)
             + jnp.dot(a_hi, wb_ref[0], preferred_element_type=F32))
        u = (jnp.dot(a_lo, wa_ref[p, 1], preferred_element_type=F32)
             + jnp.dot(a_hi, wb_ref[1], preferred_element_type=F32))
        o_ref[rs, :] = ((g / (1.0 + jnp.exp(-g))) * u).astype(o_ref.dtype)

    @pl.when(jnp.logical_and(s >= 1, m < nvalid_ref[0]))
    def _():
        _partial_tile(rows_ref[jnp.clip(m, 0, nm - 1)], a_ref.shape[0], sub, True, compute, o_ref)

    @pl.when(jnp.logical_and(s >= 1, m >= nvalid_ref[0]))
    def _():
        o_ref[...] = jnp.zeros(o_ref.shape, o_ref.dtype)


def _moe_up(a, w1, w3, *, bm, bn, sub, tile_group, n_valid, tile_rows):
    M, K = a.shape
    E, _, N = w1.shape
    nm, nn, kh = M // bm, N // bn, K // 2
    assert nm * bm == M and nn * bn == N and kh * 2 == K and kh % LANES == 0
    tile = lambda s: jnp.maximum(s - 1, 0)

    def w_ahead(n, s, g, *_):
        wrap = s == nm
        return g[jnp.where(wrap, 0, jnp.minimum(s, nm - 1))], 0, jnp.where(wrap, jnp.minimum(n + 1, nn - 1), n)

    w_now = lambda n, s, g, *_: (g[tile(s)], 1, n)
    return pl.pallas_call(
        functools.partial(_moe_up_kernel, nm=nm, kh=kh, sub=sub),
        out_shape=jax.ShapeDtypeStruct((M, N), BF16),
        grid_spec=pltpu.PrefetchScalarGridSpec(
            num_scalar_prefetch=3,
            grid=(nn, nm + 1),
            in_specs=[pl.BlockSpec((bm, K), lambda n, s, *_: (tile(s), 0)),
                      pl.BlockSpec((1, kh, bn), w_ahead), pl.BlockSpec((1, kh, bn), w_ahead),
                      pl.BlockSpec((1, kh, bn), w_now), pl.BlockSpec((1, kh, bn), w_now)],
            out_specs=pl.BlockSpec((bm, bn), lambda n, s, *_: (tile(s), n)),
            scratch_shapes=[pltpu.VMEM((2, 2, kh, bn), BF16), pltpu.VMEM((2, kh, bn), BF16),
                            pltpu.SMEM((1,), jnp.int32)]),
        compiler_params=_cparams(("arbitrary", "arbitrary")),
        name="moe_up",
    )(tile_group, n_valid, tile_rows, a, w1, w3, w1, w3)


def _rms(x, g):
    return x * lax.rsqrt(jnp.mean(x * x, axis=-1, keepdims=True) + EPS) * g


def _adaln_kernel(x_ref, g_ref, sc_ref, sh_ref, o_ref):
    y = _rms(x_ref[...], g_ref[...])
    o_ref[...] = (y * (1.0 + sc_ref[0]) + sh_ref[0]).astype(o_ref.dtype)


def _adaln_stacked_kernel(xa_ref, xb_ref, g_ref, sc_ref, sh_ref, o_ref, *, n_first):
    x = jnp.where(pl.program_id(0) < n_first, xa_ref[...], xb_ref[...])
    y = _rms(x, g_ref[...])
    o_ref[...] = (y * (1.0 + sc_ref[0]) + sh_ref[0]).astype(o_ref.dtype)


def _adaln_stacked(xa, xb, g, scale, shift, *, group_rows, bt, out_dtype):
    Ta, D = xa.shape
    T = Ta + xb.shape[0]
    n_first = Ta // bt
    assert n_first * bt == Ta and xb.shape[0] % bt == 0
    mod_spec = pl.BlockSpec((1, 1, D), lambda i: ((i * bt) // group_rows, 0, 0))
    return pl.pallas_call(
        functools.partial(_adaln_stacked_kernel, n_first=n_first), grid=(T // bt,),
        in_specs=[pl.BlockSpec((bt, D), lambda i: (jnp.minimum(i, n_first - 1), 0)),
                  pl.BlockSpec((bt, D), lambda i: (jnp.maximum(i - n_first, 0), 0)),
                  pl.BlockSpec((1, D), lambda i: (0, 0)), mod_spec, mod_spec],
        out_specs=pl.BlockSpec((bt, D), lambda i: (i, 0)),
        out_shape=jax.ShapeDtypeStruct((T, D), out_dtype),
        compiler_params=_cparams(("parallel",)), name="adaln",
    )(xa, xb, g, scale, shift)


def _rmsnorm_kernel(x_ref, g_ref, o_ref):
    o_ref[...] = _rms(x_ref[...], g_ref[...]).astype(o_ref.dtype)


def _adaln_router_kernel(x_ref, g_ref, sc_ref, sh_ref, r_ref, o_ref, idx_ref, w_ref, *, n_experts):
    y = _rms(x_ref[...], g_ref[...])
    h = y * (1.0 + sc_ref[0]) + sh_ref[0]
    o_ref[...] = h
    logits = jnp.dot(h, r_ref[...], preferred_element_type=F32, precision=lax.Precision.HIGHEST)
    lane = lax.broadcasted_iota(jnp.int32, logits.shape, 1).astype(F32)
    lg = jnp.where(lane < n_experts, logits, -jnp.inf)
    v1 = jnp.max(lg, axis=-1, keepdims=True)
    i1 = jnp.min(jnp.where(lg == v1, lane, float(LANES)), axis=-1, keepdims=True)
    lg2 = jnp.where(lane == i1, -jnp.inf, lg)
    v2 = jnp.max(lg2, axis=-1, keepdims=True)
    i2 = jnp.min(jnp.where(lg2 == v2, lane, float(LANES)), axis=-1, keepdims=True)
    e = jnp.exp(v2 - v1)
    den = 1.0 + e
    idx_ref[...] = jnp.where(lane == 0, i1, jnp.where(lane == 1, i2, 0.0)).astype(jnp.int32)
    w_ref[...] = jnp.where(lane == 0, 1.0 / den, jnp.where(lane == 1, e / den, 0.0))


def _adaln(x, g, scale, shift, *, group_rows, bt, out_dtype, router=None):
    T, D = x.shape
    grid = (T // bt,)
    x_spec = pl.BlockSpec((bt, D), lambda i: (i, 0))
    g_spec = pl.BlockSpec((1, D), lambda i: (0, 0))
    mod_spec = pl.BlockSpec((1, 1, D), lambda i: ((i * bt) // group_rows, 0, 0))
    if router is None:
        return pl.pallas_call(
            _adaln_kernel, grid=grid, in_specs=[x_spec, g_spec, mod_spec, mod_spec], out_specs=x_spec,
            out_shape=jax.ShapeDtypeStruct((T, D), out_dtype),
            compiler_params=_cparams(("parallel",)), name="adaln",
        )(x, g, scale, shift)
    n_experts = router.shape[1]
    r_pad = jnp.pad(router, ((0, 0), (0, LANES - n_experts)))
    lane_spec = pl.BlockSpec((bt, LANES), lambda i: (i, 0))
    return pl.pallas_call(
        functools.partial(_adaln_router_kernel, n_experts=n_experts), grid=grid,
        in_specs=[x_spec, g_spec, mod_spec, mod_spec, pl.BlockSpec((D, LANES), lambda i: (0, 0))],
        out_specs=[x_spec, lane_spec, lane_spec],
        out_shape=[jax.ShapeDtypeStruct((T, D), F32), jax.ShapeDtypeStruct((T, LANES), jnp.int32),
                   jax.ShapeDtypeStruct((T, LANES), F32)],
        compiler_params=_cparams(("parallel",)), name="adaln_router",
    )(x, g, scale, shift, r_pad)


def _rmsnorm(x, g, *, bt, row0, n_rows):
    D = x.shape[1]
    r0 = row0 // bt
    assert r0 * bt == row0 and n_rows % bt == 0
    return pl.pallas_call(
        _rmsnorm_kernel, grid=(n_rows // bt,),
        in_specs=[pl.BlockSpec((bt, D), lambda i: (r0 + i, 0)), pl.BlockSpec((1, D), lambda i: (0, 0))],
        out_specs=pl.BlockSpec((bt, D), lambda i: (i, 0)), out_shape=jax.ShapeDtypeStruct((n_rows, D), F32),
        compiler_params=_cparams(("parallel",)), name="final_rmsnorm",
    )(x, g)


def _qk_prep_kernel(q0_ref, q1_ref, q2_ref, k_ref, cos_ref, sin_ref, qn_ref, kn_ref,
                    qo_ref, ko_ref, ks_ref):
    cos = cos_ref[...]
    sin = sin_ref[...]
    lane = lax.broadcasted_iota(jnp.int32, cos.shape, 1)
    first_half = (lane % (HEAD_DIM // 2)) < (HEAD_DIM // 4)

    def rope(y):
        partner = jnp.where(first_half, pltpu.roll(y, HEAD_DIM - HEAD_DIM // 4, 1),
                            pltpu.roll(y, HEAD_DIM // 4, 1))
        return y * cos + partner * sin

    q_heads_per_ref = q0_ref.shape[1] // HEAD_DIM
    for r, q_ref in enumerate((q0_ref, q1_ref, q2_ref)):
        for h in range(q_heads_per_ref):
            sl = slice(h * HEAD_DIM, (h + 1) * HEAD_DIM)
            y = _rms(q_ref[:, sl], qn_ref[...])
            col = (r * q_heads_per_ref + h) * HEAD_DIM
            qo_ref[:, col:col + HEAD_DIM] = rope(y).astype(qo_ref.dtype)
    for h in range(k_ref.shape[1] // HEAD_DIM):
        sl = slice(h * HEAD_DIM, (h + 1) * HEAD_DIM)
        y = _rms(k_ref[:, sl], kn_ref[...])
        ks_ref[:, sl] = y
        ko_ref[:, sl] = rope(y).astype(ko_ref.dtype)


def _qk_prep(u, cos, sin, q_norm, k_norm, *, pool_w, q_w, kv_w, bt):
    T = u.shape[0]
    assert q_w == 3 * kv_w and pool_w == kv_w
    cb = kv_w
    row = lambda i: (i, 0)
    in_specs = [pl.BlockSpec((bt, cb), lambda i, j=j: (i, j)) for j in (1, 2, 3, 4)]
    in_specs += [pl.BlockSpec((bt, HEAD_DIM), row), pl.BlockSpec((bt, HEAD_DIM), row),
                 pl.BlockSpec((1, HEAD_DIM), lambda i: (0, 0)), pl.BlockSpec((1, HEAD_DIM), lambda i: (0, 0))]
    return pl.pallas_call(
        _qk_prep_kernel, grid=(T // bt,), in_specs=in_specs,
        out_specs=[pl.BlockSpec((bt, q_w), row), pl.BlockSpec((bt, kv_w), row), pl.BlockSpec((bt, kv_w), row)],
        out_shape=[jax.ShapeDtypeStruct((T, q_w), BF16), jax.ShapeDtypeStruct((T, kv_w), BF16),
                   jax.ShapeDtypeStruct((T, kv_w), F32)],
        compiler_params=_cparams(("parallel",)), name="qk_norm_rope",
    )(u, u, u, u, cos, sin, q_norm, k_norm)


def _attn_kernel(q_ref, k_ref, v_ref, o_ref, *, heads, groups, scale):
    for h in range(heads):
        kv_sl = slice(h * HEAD_DIM, (h + 1) * HEAD_DIM)
        k = k_ref[:, kv_sl].astype(BF16)
        v = v_ref[:, kv_sl].astype(BF16)
        for g in range(groups):
            c = (h * groups + g) * HEAD_DIM
            q = q_ref[:, c:c + HEAD_DIM].astype(BF16)
            s = lax.dot_general(q, k, (((1,), (1,)), ((), ())), preferred_element_type=F32) * scale
            p = jnp.exp2(s - jnp.max(s, axis=-1, keepdims=True))
            l = jnp.sum(p, axis=-1, keepdims=True)
            o = jnp.dot(p.astype(BF16), v, preferred_element_type=F32) / l
            o_ref[:, c:c + HEAD_DIM] = o.astype(o_ref.dtype)


def _attention(q, k, v, *, batch, s_len, t_len, kv_heads, groups, bq, hp, q_row0=0, q_col0=0,
               k_row0=0, k_col0=0, v_col0=0):
    qw, kw = hp * groups * HEAD_DIM, hp * HEAD_DIM
    nq = s_len // bq
    assert q_row0 % bq == 0 and q_col0 % qw == 0 and k_row0 % t_len == 0 and kv_heads % hp == 0
    assert k_col0 % kw == 0 and v_col0 % kw == 0
    qr, qc, kr, kc, vc = q_row0 // bq, q_col0 // qw, k_row0 // t_len, k_col0 // kw, v_col0 // kw
    return pl.pallas_call(
        functools.partial(_attn_kernel, heads=hp, groups=groups, scale=HEAD_DIM ** -0.5 * LOG2E),
        grid=(batch, kv_heads // hp, nq),
        in_specs=[pl.BlockSpec((bq, qw), lambda b, h, i: (qr + b * nq + i, qc + h)),
                  pl.BlockSpec((t_len, kw), lambda b, h, i: (kr + b, kc + h)),
                  pl.BlockSpec((t_len, kw), lambda b, h, i: (kr + b, vc + h))],
        out_specs=pl.BlockSpec((bq, qw), lambda b, h, i: (b * nq + i, h)),
        out_shape=jax.ShapeDtypeStruct((batch * s_len, kv_heads * groups * HEAD_DIM), BF16),
        compiler_params=_cparams(("parallel", "parallel", "arbitrary")), name="attention",
    )(q, k, v)


def _pool_kernel(u_ref, w_ref, s_ref, o_ref, pad_ref, *, seq):
    g = pl.program_id(1)
    x = u_ref[...]
    pad_ref[0:POOL_HALO, :] = jnp.zeros((POOL_HALO, x.shape[1]), F32)
    pad_ref[POOL_HALO + seq:, :] = jnp.zeros((POOL_HALO, x.shape[1]), F32)
    pad_ref[POOL_HALO:POOL_HALO + seq, :] = x
    t = lax.broadcasted_iota(jnp.int32, x.shape, 0)
    for gi, win in enumerate(POOL_WINDOWS):
        @pl.when(g == gi)
        def _(win=win):
            back, fwd = win // 2, win - win // 2
            tot = pad_ref[POOL_HALO - back:POOL_HALO - back + seq, :]
            for j in range(-back + 1, fwd):
                tot = tot + pad_ref[POOL_HALO + j:POOL_HALO + j + seq, :]
            cnt = (jnp.minimum(t + fwd, seq) - jnp.maximum(t - back, 0)).astype(F32)
            diff = (tot / cnt - x).astype(BF16)
            y = jnp.dot(diff, w_ref[0], preferred_element_type=F32) * s_ref[...]
            o_ref[...] = y.astype(o_ref.dtype)


def _pool_mix(u, w_pool, pool_scale, *, n_seq, seq, row0):
    n_groups, cg, _ = w_pool.shape
    assert row0 % seq == 0 and max(POOL_WINDOWS) // 2 <= POOL_HALO and seq % 8 == 0
    r0 = row0 // seq
    return pl.pallas_call(
        functools.partial(_pool_kernel, seq=seq), grid=(n_seq, n_groups),
        in_specs=[pl.BlockSpec((seq, cg), lambda s, g: (r0 + s, g)),
                  pl.BlockSpec((1, cg, cg), lambda s, g: (g, 0, 0)),
                  pl.BlockSpec((1, cg), lambda s, g: (0, g))],
        out_specs=pl.BlockSpec((seq, cg), lambda s, g: (s, g)),
        out_shape=jax.ShapeDtypeStruct((n_seq * seq, n_groups * cg), BF16),
        scratch_shapes=[pltpu.VMEM((seq + 2 * POOL_HALO, cg), F32)],
        compiler_params=_cparams(("parallel", "arbitrary")), name="pool_mix",
    )(u, w_pool, pool_scale)


def _na_kernel(q_ref, k_ref, v_ref, kc_ref, vc_ref, t_ref, o_ref, bias_ref, *,
               rows_per_blk, key_rows, n_rows, wr, n_blk, scale):
    blk = pl.program_id(2)
    half = (key_rows - rows_per_blk) // 2
    W = GRID_W
    lane = lax.broadcasted_iota(jnp.int32, (W, 2 * W), 1)

    def build(b):
        k_start = min(max(b * rows_per_blk - half, 0), n_rows - key_rows)
        for qr in range(rows_per_blk):
            r = b * rows_per_blk + qr
            rs = min(max(r - wr // 2, 0), n_rows - wr)
            for kp in range(key_rows // 2):
                idx = [kk - r + NA_ROWS - 1 if rs <= kk < rs + wr else 2 * NA_ROWS - 1
                       for kk in (k_start + 2 * kp, k_start + 2 * kp + 1)]
                bias_ref[qr * W:(qr + 1) * W, kp * 2 * W:(kp + 1) * 2 * W] = jnp.where(
                    lane < W, t_ref[0, idx[0]], t_ref[0, idx[1]])

    pl.when(blk == 0)(functools.partial(build, 0))
    if n_blk > 2:
        pl.when(blk == 1)(functools.partial(build, 1))
    if n_blk > 1:
        pl.when(blk == n_blk - 1)(functools.partial(build, n_blk - 1))

    k_start = jnp.clip(blk * rows_per_blk - half, 0, n_rows - key_rows)
    tok0 = pl.multiple_of(k_start * W, W * 4)
    n_keys = key_rows * W
    q = q_ref[...].astype(BF16)
    kl = k_ref[pl.ds(tok0, n_keys), :].astype(BF16)
    vl = v_ref[pl.ds(tok0, n_keys), :].astype(BF16)
    kc = kc_ref[...].astype(BF16)
    vc = vc_ref[...].astype(BF16)
    dn = (((1,), (1,)), ((), ()))
    s_loc = lax.dot_general(q, kl, dn, preferred_element_type=F32) * scale + bias_ref[...]
    s_ctx = lax.dot_general(q, kc, dn, preferred_element_type=F32) * scale
    m = jnp.maximum(jnp.max(s_loc, axis=-1, keepdims=True), jnp.max(s_ctx, axis=-1, keepdims=True))
    p_loc = jnp.exp2(s_loc - m)
    p_ctx = jnp.exp2(s_ctx - m)
    l = jnp.sum(p_loc, axis=-1, keepdims=True) + jnp.sum(p_ctx, axis=-1, keepdims=True)
    o = (jnp.dot(p_ctx.astype(BF16), vc, preferred_element_type=F32)
         + jnp.dot(p_loc.astype(BF16), vl, preferred_element_type=F32)) / l
    o_ref[...] = o.astype(o_ref.dtype)


def _na_col_tables(rpb):
    H = rpb.shape[0]
    col = jnp.arange(GRID_W)
    cstart = jnp.clip(col - NA_COLS // 2, 0, GRID_W - NA_COLS)
    col_valid = (col[None, :] >= cstart[:, None]) & (col[None, :] < cstart[:, None] + NA_COLS)
    col_idx = jnp.clip(col[None, :] - col[:, None] + NA_COLS - 1, 0, 2 * NA_COLS - 2)
    onehot = (col_idx[None] == jnp.arange(2 * NA_COLS - 1)[:, None, None]).astype(F32)
    t = jnp.einsum("hrc,cqk->hrqk", rpb.astype(F32), onehot, precision=lax.Precision.HIGHEST)
    t = jnp.where(col_valid[None, None], t * LOG2E, NEG_INF)
    t = jnp.concatenate([t, jnp.full((H, 1, GRID_W, GRID_W), NEG_INF, F32)], axis=1)
    return jnp.concatenate([t, t], axis=-1)


def _na_attention(qkv, k_ctx, v_ctx, rpb, *, batch, n_tok, heads, row0, t_ctx):
    n_rows = n_tok // GRID_W
    wr = min(NA_ROWS, n_rows)
    rows_per_blk = min(NA_ROWS, n_rows)
    key_rows = min(rows_per_blk + wr, n_rows)
    n_blk = n_rows // rows_per_blk
    bq = rows_per_blk * GRID_W
    assert n_rows % rows_per_blk == 0 and row0 % n_tok == 0 and row0 % bq == 0 and key_rows % 2 == 0
    assert rows_per_blk % 4 == 0 and ((key_rows - rows_per_blk) // 2) % 4 == 0
    tables = _na_col_tables(rpb)
    qr0, kr0 = row0 // bq, row0 // n_tok
    return pl.pallas_call(
        functools.partial(_na_kernel, rows_per_blk=rows_per_blk, key_rows=key_rows, n_rows=n_rows, wr=wr,
                          n_blk=n_blk, scale=HEAD_DIM ** -0.5 * LOG2E),
        grid=(heads, batch, n_blk),
        in_specs=[pl.BlockSpec((bq, HEAD_DIM), lambda h, b, i: (qr0 + b * n_blk + i, h)),
                  pl.BlockSpec((n_tok, HEAD_DIM), lambda h, b, i: (kr0 + b, heads + h)),
                  pl.BlockSpec((n_tok, HEAD_DIM), lambda h, b, i: (kr0 + b, 2 * heads + h)),
                  pl.BlockSpec((t_ctx, HEAD_DIM), lambda h, b, i: (b, h)),
                  pl.BlockSpec((t_ctx, HEAD_DIM), lambda h, b, i: (b, h)),
                  pl.BlockSpec((1, 2 * NA_ROWS, GRID_W, 2 * GRID_W), lambda h, b, i: (h, 0, 0, 0))],
        out_specs=pl.BlockSpec((bq, HEAD_DIM), lambda h, b, i: (b * n_blk + i, h)),
        out_shape=jax.ShapeDtypeStruct((batch * n_tok, heads * HEAD_DIM), BF16),
        scratch_shapes=[pltpu.VMEM((bq, key_rows * GRID_W), F32)],
        compiler_params=_cparams(("arbitrary", "arbitrary", "arbitrary")), name="na_attention",
    )(qkv, qkv, qkv, k_ctx, v_ctx, tables)


def _row_copy(src_hbm, row, dst, slot, sem):
    return pltpu.make_async_copy(src_hbm.at[pl.ds(row, 1), :], dst.at[pl.ds(slot, 1), :], sem)


def _gather_kernel(src_ref, x_hbm, o_ref, buf, sem, *, rt, n_steps):
    i = pl.program_id(0)
    cur = i % 2

    def issue(step, s):
        def body(r, c):
            _row_copy(x_hbm, src_ref[step * rt + r], buf.at[s], r, sem.at[s]).start()
            return c
        lax.fori_loop(0, rt, body, 0, unroll=8)

    @pl.when(i == 0)
    def _():
        issue(0, 0)

    @pl.when(i + 1 < n_steps)
    def _():
        issue(i + 1, 1 - cur)

    pltpu.make_async_copy(x_hbm.at[pl.ds(0, rt), :], buf.at[cur], sem.at[cur]).wait()
    o_ref[...] = buf[cur].astype(o_ref.dtype)


def _gather_rows(x, src, *, rt):
    n = src.shape[0]
    D = x.shape[1]
    return pl.pallas_call(
        functools.partial(_gather_kernel, rt=rt, n_steps=n // rt),
        out_shape=jax.ShapeDtypeStruct((n, D), BF16),
        grid_spec=pltpu.PrefetchScalarGridSpec(
            num_scalar_prefetch=1, grid=(n // rt,),
            in_specs=[pl.BlockSpec(memory_space=pl.ANY)],
            out_specs=pl.BlockSpec((rt, D), lambda i, s: (i, 0)),
            scratch_shapes=[pltpu.VMEM((2, rt, D), F32), pltpu.SemaphoreType.DMA((2,))]),
        compiler_params=_cparams(("arbitrary",)), name="moe_dispatch",
    )(src, x)


def _combine_kernel(pos_ref, x_ref, g_ref, w_ref, y_hbm, o_ref, buf, sem, *, ct, n_steps):
    i = pl.program_id(0)
    cur = i % 2

    def issue(step, s):
        def body(r, c):
            for kk in range(TOP_K):
                _row_copy(y_hbm, pos_ref[TOP_K * (step * ct + r) + kk], buf.at[s, kk], r, sem.at[s]).start()
            return c
        lax.fori_loop(0, ct, body, 0, unroll=8)

    @pl.when(i == 0)
    def _():
        issue(0, 0)

    @pl.when(i + 1 < n_steps)
    def _():
        issue(i + 1, 1 - cur)

    for kk in range(TOP_K):
        pltpu.make_async_copy(y_hbm.at[pl.ds(0, ct), :], buf.at[cur, kk], sem.at[cur]).wait()
    w = w_ref[...]
    acc = w[:, 0:1] * buf[cur, 0]
    for kk in range(1, TOP_K):
        acc = acc + w[:, kk:kk + 1] * buf[cur, kk]
    o_ref[...] = x_ref[...] + g_ref[0] * acc


def _combine(x, gate, top_w, y, pos, *, ct, gate_rows):
    T, D = x.shape
    return pl.pallas_call(
        functools.partial(_combine_kernel, ct=ct, n_steps=T // ct),
        out_shape=jax.ShapeDtypeStruct((T, D), F32),
        grid_spec=pltpu.PrefetchScalarGridSpec(
            num_scalar_prefetch=1, grid=(T // ct,),
            in_specs=[pl.BlockSpec((ct, D), lambda i, p: (i, 0)),
                      pl.BlockSpec((1, 1, D), lambda i, p: ((i * ct) // gate_rows, 0, 0)),
                      pl.BlockSpec((ct, LANES), lambda i, p: (i, 0)),
                      pl.BlockSpec(memory_space=pl.ANY)],
            out_specs=pl.BlockSpec((ct, D), lambda i, p: (i, 0)),
            scratch_shapes=[pltpu.VMEM((2, TOP_K, ct, D), F32), pltpu.SemaphoreType.DMA((2,))]),
        compiler_params=_cparams(("arbitrary",)), name="moe_combine",
    )(pos, x, gate, top_w, y)


def _routing_plan(top_i, n_experts, tm):
    T = top_i.shape[0]
    e_flat = top_i.reshape(-1)
    onehot = (e_flat[:, None] == jnp.arange(n_experts)[None, :]).astype(jnp.int32)
    counts = jnp.sum(onehot, axis=0)
    rank = jnp.sum((jnp.cumsum(onehot, axis=0) - onehot) * onehot, axis=1)
    gsz = ((counts + tm - 1) // tm) * tm
    gend = jnp.cumsum(gsz)
    goff = gend - gsz
    pos = (goff[e_flat] + rank).astype(jnp.int32)
    n_tiles = (T * TOP_K) // tm + n_experts
    src = jnp.zeros((n_tiles * tm,), jnp.int32).at[pos].set(jnp.arange(T * TOP_K, dtype=jnp.int32) // TOP_K)
    n_valid = (gend[-1] // tm).astype(jnp.int32)
    tile_row = jnp.minimum(jnp.arange(n_tiles, dtype=jnp.int32), n_valid - 1) * tm
    tile_expert = jnp.sum((tile_row[:, None] >= gend[None, :]).astype(jnp.int32), axis=1)
    tile_rows = jnp.clip((goff + counts)[tile_expert] - tile_row, 0, tm)
    return pos, src, tile_expert.astype(jnp.int32), n_valid.reshape(1), tile_rows.astype(jnp.int32)


def _pick(n, prefs):
    for p in prefs:
        if n % p == 0:
            return p
    return n


def _modulation(cvecs, w, b):
    n_c, D = cvecs.shape
    rows = 8
    a = jnp.zeros((rows, D), BF16).at[:n_c].set(jax.nn.silu(cvecs).astype(BF16))
    m = _matmul(a, (w[None],), bm=rows, bn=_pick(6 * D, (1024, 512, 256, 128)), bk=D,
                mode="bias", extra=(b[None, :],), name="modulation")
    return jnp.transpose(m[:n_c].reshape(n_c, 6, 1, D), (1, 0, 2, 3))


def _rope_tables(n_prompt_rows, dec_batch, dec_seq):
    t = jnp.arange(dec_seq)
    pos = jnp.stack([t // GRID_W, t % GRID_W], axis=-1).astype(F32)
    n_freq = HEAD_DIM // 4
    inv = ROPE_THETA ** (-jnp.arange(n_freq, dtype=F32) / n_freq)
    ang = pos[:, :, None] * inv
    cos = jnp.concatenate([jnp.cos(ang)] * 2, axis=-1).reshape(dec_seq, HEAD_DIM)
    sin = jnp.concatenate([-jnp.sin(ang), jnp.sin(ang)], axis=-1).reshape(dec_seq, HEAD_DIM)
    cos = jnp.concatenate([jnp.ones((n_prompt_rows, HEAD_DIM), F32)] + [cos] * dec_batch)
    sin = jnp.concatenate([jnp.zeros((n_prompt_rows, HEAD_DIM), F32)] + [sin] * dec_batch)
    return cos, sin


def kernel(x_prompt, x_sample, c, c_ctx, cache_l0_attn_k, cache_l0_attn_v, cache_l1_na_k, cache_l1_na_v,
           l0_ada_w, l0_ada_b, l0_norm1, l0_w_in, l0_q_norm, l0_k_norm, l0_w_pool, l0_pool_scale, l0_w_out,
           l0_norm2, l0_ffn_w1, l0_ffn_w3, l0_ffn_w2,
           l1_ada_w, l1_ada_b, l1_norm1, l1_w_qkv, l1_rpb, l1_w_out, l1_norm2, l1_router,
           l1_exp_w1, l1_exp_w3, l1_exp_w2, final_norm):
    batch, seq, D = x_prompt.shape
    dec_batch, dec_seq, _ = x_sample.shape
    past_len = cache_l0_attn_k.shape[1]
    kv_heads0 = cache_l0_attn_k.shape[2]
    heads1 = cache_l1_na_k.shape[2]
    n_pool, cg, _ = l0_w_pool.shape
    pool_w = n_pool * cg
    kv_w = kv_heads0 * HEAD_DIM
    q_w = l0_w_in.shape[1] - pool_w - 2 * kv_w
    groups0 = q_w // kv_w
    c_w = heads1 * HEAD_DIM
    n_experts = l1_router.shape[1]
    Tp, Ts = batch * seq, dec_batch * dec_seq
    T = Tp + Ts
    group_rows = math.gcd(Tp, dec_seq)
    grp_to_c = jnp.array([0] * (Tp // group_rows) + [1 + b for b in range(dec_batch)
                                                      for _ in range(dec_seq // group_rows)], jnp.int32)

    bt = _pick(group_rows, (256, 128, 64, 32, 16, 8))
    bm = _pick(group_rows, (1024, 512, 256, 128, 64, 32, 16, 8))
    blk = lambda n: _pick(n, (512, 256, 128))
    seq_blk = lambda n: _pick(n, (256, 128, 64, 32, 16, 8))
    mods = lambda m: [m[i][grp_to_c] for i in range(6)]
    cvecs = jnp.concatenate([c_ctx[None, :], c], axis=0)
    gg = dict(gate_rows=group_rows)

    xa, xb = x_prompt.reshape(Tp, D), x_sample.reshape(Ts, D)

    sh1, sc1, g1, sh2, sc2, g2 = mods(_modulation(cvecs, l0_ada_w, l0_ada_b))
    h = _adaln_stacked(xa, xb, l0_norm1[None], sc1, sh1, group_rows=group_rows, bt=bt, out_dtype=BF16)
    u = _matmul(h, (l0_w_in[None],), bm=bm, bn=blk(l0_w_in.shape[1]), bk=D, name="l0_in_proj")
    cos, sin = _rope_tables(Tp, dec_batch, dec_seq)
    q_att, k_att, k_state = _qk_prep(u, cos, sin, l0_q_norm[None], l0_k_norm[None],
                                     pool_w=pool_w, q_w=q_w, kv_w=kv_w, bt=bt)
    v_col0 = pool_w + q_w + kv_w
    att_p = _attention(q_att, k_att, u, batch=batch, s_len=seq, t_len=seq, kv_heads=kv_heads0, groups=groups0,
                       bq=seq_blk(seq), hp=kv_heads0, v_col0=v_col0)
    k_lat = jnp.concatenate([cache_l0_attn_k.reshape(dec_batch, past_len, kv_w).astype(BF16),
                             k_att[Tp:].reshape(dec_batch, dec_seq, kv_w)], axis=1).reshape(-1, kv_w)
    v_lat = jnp.concatenate([cache_l0_attn_v.reshape(dec_batch, past_len, kv_w),
                             u[Tp:, v_col0:].reshape(dec_batch, dec_seq, kv_w)], axis=1).reshape(-1, kv_w)
    att_s = _attention(q_att, k_lat, v_lat, batch=dec_batch, s_len=dec_seq, t_len=past_len + dec_seq,
                       kv_heads=kv_heads0, groups=groups0, bq=seq_blk(dec_seq), hp=1, q_row0=Tp)
    w_pool = l0_w_pool.astype(BF16)
    pool_p = _pool_mix(u, w_pool, l0_pool_scale[None], n_seq=batch, seq=seq, row0=0)
    pool_s = _pool_mix(u, w_pool, l0_pool_scale[None], n_seq=dec_batch, seq=dec_seq, row0=Tp)
    mix = jnp.concatenate([jnp.concatenate([pool_p, pool_s], axis=0),
                           jnp.concatenate([att_p, att_s], axis=0)], axis=1)
    x = _matmul(mix, (l0_w_out[None],), bm=bm, bn=blk(D), bk=mix.shape[1], mode="gated_res",
                extra=(xa, xb, g1), n_first=Tp // bm, name="l0_out_proj", **gg)
    h = _adaln(x, l0_norm2[None], sc2, sh2, group_rows=group_rows, bt=bt, out_dtype=BF16)
    d_ff = l0_ffn_w1.shape[1]
    hh = _matmul(h, (l0_ffn_w1[None], l0_ffn_w3[None]), bm=bm, bn=blk(d_ff), bk=D,
                 mode="swiglu", out_dtype=BF16, name="l0_ffn_up")
    x = _matmul(hh, (l0_ffn_w2.astype(BF16)[None],), bm=bm, bn=blk(D),
                bk=_pick(d_ff, (5504, 2048, 1024, 512, 256, 128)),
                mode="gated_res", extra=(x, g2), name="l0_ffn_down", **gg)
    state_l0_k = k_state[:Tp].reshape(batch, seq, kv_heads0, HEAD_DIM)
    state_l0_v = u[:Tp, v_col0:].reshape(batch, seq, kv_heads0, HEAD_DIM)

    sh1, sc1, g1, sh2, sc2, g2 = mods(_modulation(cvecs, l1_ada_w, l1_ada_b))
    h = _adaln(x, l1_norm1[None], sc1, sh1, group_rows=group_rows, bt=bt, out_dtype=BF16)
    qkv = _matmul(h, (l1_w_qkv[None],), bm=bm, bn=blk(3 * c_w), bk=D, name="l1_qkv_proj")
    att_p = _attention(qkv, qkv, qkv, batch=batch, s_len=seq, t_len=seq, kv_heads=heads1, groups=1,
                       bq=seq_blk(seq), hp=_pick(heads1, (8, 4, 2, 1)), k_col0=c_w, v_col0=2 * c_w)
    att_s = _na_attention(qkv, cache_l1_na_k.reshape(dec_batch * past_len, c_w),
                          cache_l1_na_v.reshape(dec_batch * past_len, c_w), l1_rpb,
                          batch=dec_batch, n_tok=dec_seq, heads=heads1, row0=Tp, t_ctx=past_len)
    x = _matmul(jnp.concatenate([att_p, att_s], axis=0), (l1_w_out[None],), bm=bm, bn=blk(D), bk=c_w,
                mode="gated_res", extra=(x, g1), name="l1_out_proj", **gg)
    h32, top_i, top_w = _adaln(x, l1_norm2[None], sc2, sh2, group_rows=group_rows, bt=bt, out_dtype=F32,
                               router=l1_router)
    tm = _pick(T * TOP_K, (512, 256, 128, 64, 32, 16, 8))
    sub = LANES if tm % LANES == 0 else tm
    pos, src, tile_expert, n_valid, tile_rows = _routing_plan(top_i[:, :TOP_K], n_experts, tm)
    routed = dict(tile_group=tile_expert, n_valid=n_valid, tile_rows=tile_rows, sub=sub)
    xs = _gather_rows(h32, src, rt=_pick(tm, (256, 128, 64, 32, 16, 8)))
    d_fe = l1_exp_w1.shape[2]
    he = _moe_up(xs, l1_exp_w1, l1_exp_w3, bm=tm, bn=blk(d_fe), **routed)
    ye = _matmul(he, (l1_exp_w2.astype(BF16),), bm=tm, bn=_pick(D, (1024, 512, 256, 128)),
                 bk=_pick(d_fe, (7168, 3584, 2048, 1024, 512, 256, 128)), name="moe_down", **routed)
    x = _combine(x, g2, top_w, ye, pos, ct=_pick(group_rows, (128, 64, 32, 16, 8)), gate_rows=group_rows)
    state_l1_k = qkv[:Tp, c_w:2 * c_w].reshape(batch, seq, heads1, HEAD_DIM)
    state_l1_v = qkv[:Tp, 2 * c_w:].reshape(batch, seq, heads1, HEAD_DIM)

    y_p = _rmsnorm(x, final_norm[None], bt=bt, row0=0, n_rows=Tp)
    y_s = _rmsnorm(x, final_norm[None], bt=bt, row0=Tp, n_rows=Ts)
    return (y_p.reshape(batch, seq, D), y_s.reshape(dec_batch, dec_seq, D),
            state_l0_k, state_l0_v, state_l1_k, state_l1_v)
```

```python
import functools
import math

import jax
import jax.numpy as jnp
from jax import lax
from jax.experimental import pallas as pl
from jax.experimental.pallas import tpu as pltpu

HEAD_DIM = 128
GRID_W = 64
POOL_WINDOWS = (2, 4, 8, 16)
NA_ROWS = 8
NA_COLS = 16
TOP_K = 2
EPS = 1e-6
ROPE_THETA = 10000.0
NEG_INF = -1e30
LOG2E = math.log2(math.e)
LANES = 128
POOL_HALO = 16
VMEM_LIMIT_BYTES = 60 * 1024 * 1024

F32 = jnp.float32
BF16 = jnp.bfloat16


def _cparams(sem):
    return pltpu.CompilerParams(dimension_semantics=sem, vmem_limit_bytes=VMEM_LIMIT_BYTES)


def _partial_tile(rows, bm, sub, k_last, compute, o_ref):
    nearly_full = rows > bm - sub

    @pl.when(nearly_full)
    def _():
        compute(slice(0, bm))

    @pl.when(jnp.logical_not(nearly_full))
    def _():
        for j in range(bm // sub):
            c = slice(j * sub, (j + 1) * sub)
            if j < bm // sub - 1:
                pl.when(rows > j * sub)(functools.partial(compute, c))

            @pl.when(jnp.logical_and(rows <= j * sub, k_last))
            def _(c=c):
                o_ref[c, :] = jnp.zeros((sub, o_ref.shape[1]), o_ref.dtype)


def _mm_kernel(grp_ref, nvalid_ref, rows_ref, *refs, nk, mode, cast_w, sub, n_first, seg_stacked, res_stacked,
               side):
    n_w = 2 if mode == "swiglu" else 1
    n_a = sum(2 if st else 1 for st in seg_stacked)
    a_refs = refs[:n_a]
    w_refs = refs[n_a:n_a + n_w]
    n_extra = {"plain": 0, "swiglu": 0, "bias": 1, "gated_res": 3 if res_stacked else 2}[mode]
    pos = n_a + n_w
    extra = refs[pos:pos + n_extra]
    pos += n_extra
    if side:
        side_in, o_ref, side_out = refs[pos], refs[pos + 1], refs[pos + 2]
        pos += 3
    else:
        o_ref = refs[pos]
        pos += 1
    scratch = refs[pos:]
    bm = a_refs[0].shape[0]
    m = pl.program_id(1)
    k = pl.program_id(2)

    def a_segments(rs):
        out, i = [], 0
        for st in seg_stacked:
            if st:
                out.append(jnp.where(m < n_first, a_refs[i][rs, :], a_refs[i + 1][rs, :]))
                i += 2
            else:
                out.append(a_refs[i][rs, :])
                i += 1
        return out

    def epilogue(accs, rs):
        if mode == "plain":
            r = accs[0]
        elif mode == "swiglu":
            g = accs[0]
            r = (g / (1.0 + jnp.exp(-g))) * accs[1]
        elif mode == "bias":
            r = accs[0] + extra[0][...]
        else:
            res = jnp.where(m < n_first, extra[0][rs, :], extra[1][rs, :]) if res_stacked else extra[0][rs, :]
            r = res + extra[-1][0] * accs[0]
        o_ref[rs, :] = r.astype(o_ref.dtype)

    def compute(rs):
        w_of = (lambda j: scratch[j]) if cast_w else (lambda j: w_refs[j].at[0])
        segs = a_segments(rs)
        parts = []
        for j in range(n_w):
            acc, off = None, 0
            for a in segs:
                d = jnp.dot(a, w_of(j)[off:off + a.shape[1], :], preferred_element_type=F32)
                acc = d if acc is None else acc + d
                off += a.shape[1]
            parts.append(acc)
        if side:
            side_out[...] = side_in[...].astype(BF16)
        if nk == 1:
            epilogue(parts, rs)
        else:
            @pl.when(k == 0)
            def _():
                for acc, p in zip(scratch, parts):
                    acc[rs, :] = p

            @pl.when(k > 0)
            def _():
                for acc, p in zip(scratch, parts):
                    acc[rs, :] += p

            @pl.when(k == nk - 1)
            def _():
                epilogue([acc[rs, :] for acc in scratch], rs)

    @pl.when(m < nvalid_ref[0])
    def _():
        if cast_w:
            @pl.when(jnp.logical_or(m == 0, grp_ref[m] != grp_ref[jnp.maximum(m - 1, 0)]))
            def _():
                for w_ref, wb in zip(w_refs, scratch):
                    wb[...] = w_ref[0].astype(BF16)
        if sub is None:
            compute(slice(0, bm))
        else:
            _partial_tile(rows_ref[m], bm, sub, k == nk - 1, compute, o_ref)

    @pl.when(jnp.logical_and(m >= nvalid_ref[0], k == nk - 1))
    def _():
        o_ref[...] = jnp.zeros(o_ref.shape, o_ref.dtype)


def _matmul(a, ws, *, bm, bn, bk, mode="plain", out_dtype=F32, extra=(), tile_group=None,
            n_valid=None, tile_rows=None, sub=None, gate_rows=None, n_first=None, side=None, name="mm"):
    segs = a if isinstance(a, (list, tuple)) else [a]
    seg_stacked = tuple(isinstance(sg, tuple) for sg in segs)
    seg_rows = lambda sg: sum(p.shape[0] for p in sg) if isinstance(sg, tuple) else sg.shape[0]
    seg_cols = lambda sg: sg[0].shape[1] if isinstance(sg, tuple) else sg.shape[1]
    M, K = seg_rows(segs[0]), sum(seg_cols(sg) for sg in segs)
    E, _, N = ws[0].shape
    nm, nn, nk = M // bm, N // bn, K // bk
    assert nm * bm == M and nn * bn == N and nk * bk == K, (M, K, ws[0].shape, bm, bn, bk)
    assert all(seg_rows(sg) == M for sg in segs) and (nk == 1 or len(segs) == 1 and not seg_stacked[0])
    cast_w = ws[0].dtype == F32
    assert not (cast_w and nk > 1) and not (side is not None and (sub is not None or nk > 1))
    if tile_group is None:
        tile_group = jnp.zeros((nm,), jnp.int32)
        n_valid = jnp.full((1,), nm, jnp.int32)
    if tile_rows is None:
        tile_rows = jnp.full((nm,), bm, jnp.int32)
    kk = lambda m, k: jnp.where(m % 2 == 1, nk - 1 - k, k) if nk > 1 else k
    first_rows = lambda n, m, k, *_: (jnp.minimum(m, n_first - 1), 0)
    second_rows = lambda n, m, k, *_: (jnp.maximum(m - n_first, 0), 0)
    in_specs, a_args = [], []
    for sg in segs:
        if isinstance(sg, tuple):
            assert sg[0].shape[0] == n_first * bm
            in_specs += [pl.BlockSpec((bm, sg[0].shape[1]), first_rows), pl.BlockSpec((bm, sg[1].shape[1]), second_rows)]
            a_args += list(sg)
        else:
            in_specs.append(pl.BlockSpec((bm, bk if nk > 1 else sg.shape[1]), lambda n, m, k, *_: (m, kk(m, k))))
            a_args.append(sg)
    for _ in ws:
        in_specs.append(pl.BlockSpec((1, bk, bn), lambda n, m, k, g, *_: (g[m], kk(m, k), n)))
    if mode == "bias":
        in_specs.append(pl.BlockSpec((1, bn), lambda n, m, k, *_: (0, n)))
    elif mode == "gated_res":
        if len(extra) == 2:
            in_specs.append(pl.BlockSpec((bm, bn), lambda n, m, k, *_: (m, n)))
        else:
            in_specs.append(pl.BlockSpec((bm, bn), lambda n, m, k, *_: (jnp.minimum(m, n_first - 1), n)))
            in_specs.append(pl.BlockSpec((bm, bn), lambda n, m, k, *_: (jnp.maximum(m - n_first, 0), n)))
        in_specs.append(pl.BlockSpec((1, 1, bn), lambda n, m, k, *_: ((m * bm) // gate_rows, 0, n)))
    if cast_w:
        scratch = [pltpu.VMEM((bk, bn), BF16) for _ in ws]
    else:
        scratch = [] if nk == 1 else [pltpu.VMEM((bm, bn), F32) for _ in ws]
    out_shape = jax.ShapeDtypeStruct((M, N), out_dtype)
    out_specs = pl.BlockSpec((bm, bn), lambda n, m, k, *_: (m, n))
    side_args = ()
    if side is not None:
        src, rows = side
        n_side = src.shape[0] // rows
        assert n_side * rows == src.shape[0] and n_side <= nn * nm
        side_spec = pl.BlockSpec((rows, src.shape[1]), lambda n, m, k, *_: (jnp.minimum(n * nm + m, n_side - 1), 0))
        in_specs.append(side_spec)
        out_shape = [out_shape, jax.ShapeDtypeStruct(src.shape, BF16)]
        out_specs = [out_specs, side_spec]
        side_args = (src,)
    return pl.pallas_call(
        functools.partial(_mm_kernel, nk=nk, mode=mode, cast_w=cast_w, sub=sub, n_first=n_first,
                          seg_stacked=seg_stacked, res_stacked=mode == "gated_res" and len(extra) == 3,
                          side=side is not None),
        out_shape=out_shape,
        grid_spec=pltpu.PrefetchScalarGridSpec(
            num_scalar_prefetch=3,
            grid=(nn, nm, nk),
            in_specs=in_specs,
            out_specs=out_specs,
            scratch_shapes=scratch),
        compiler_params=_cparams(("arbitrary", "arbitrary", "arbitrary")),
        name=name,
    )(tile_group, n_valid, tile_rows, *a_args, *ws, *extra, *side_args)


def _matmul_with_cast(a, ws, src, *, bm, bn, **kw):
    steps = (a.shape[0] // bm) * (ws[0].shape[2] // bn)
    R = src.shape[0]
    rows = next((r for r in range(16, R + 1, 16) if R % r == 0 and R // r <= steps), None)
    if rows is None:
        return _matmul(a, ws, bm=bm, bn=bn, **kw), src.astype(BF16)
    out, cast = _matmul(a, ws, bm=bm, bn=bn, side=(src, rows), **kw)
    return out, cast


def _moe_up_kernel(grp_ref, nvalid_ref, rows_ref, a_ref, w1a_ref, w3a_ref, w1b_ref, w3b_ref, o_ref,
                   wa_ref, wb_ref, par_ref, *, nm, kh, sub):
    n = pl.program_id(0)
    s = pl.program_id(1)
    m = s - 1
    first = jnp.logical_and(n == 0, s == 0)
    g_cur = grp_ref[jnp.clip(m, 0, nm - 1)]
    g_prev = grp_ref[jnp.clip(m - 1, 0, nm - 1)]
    g_next = grp_ref[jnp.clip(m + 1, 0, nm - 1)]

    @pl.when(first)
    def _():
        par_ref[0] = 0

    @pl.when(jnp.logical_or(s == 1, jnp.logical_and(s > 1, g_cur != g_prev)))
    def _():
        par_ref[0] = 1 - par_ref[0]
        wb_ref[0] = w1b_ref[0].astype(BF16)
        wb_ref[1] = w3b_ref[0].astype(BF16)

    p = par_ref[0]

    nxt_new = jnp.logical_or(jnp.logical_or(first, s == nm),
                             jnp.logical_and(jnp.logical_and(s >= 1, s < nm), g_next != g_cur))

    @pl.when(nxt_new)
    def _():
        wa_ref[1 - p, 0] = w1a_ref[0].astype(BF16)
        wa_ref[1 - p, 1] = w3a_ref[0].astype(BF16)

    def compute(rs):
        a_lo = a_ref[rs, :kh]
        a_hi = a_ref[rs, kh:]
        g = (jnp.dot(a_lo, wa_ref[p, 0], preferred_element_type=F32)
             + jnp.dot(a_hi, wb_ref[0], preferred_element_type=F32))
        u = (jnp.dot(a_lo, wa_ref[p, 1], preferred_element_type=F32)
             + jnp.dot(a_hi, wb_ref[1], preferred_element_type=F32))
        o_ref[rs, :] = ((g / (1.0 + jnp.exp(-g))) * u).astype(o_ref.dtype)

    @pl.when(jnp.logical_and(s >= 1, m < nvalid_ref[0]))
    def _():
        _partial_tile(rows_ref[jnp.clip(m, 0, nm - 1)], a_ref.shape[0], sub, True, compute, o_ref)

    @pl.when(jnp.logical_and(s >= 1, m >= nvalid_ref[0]))
    def _():
        o_ref[...] = jnp.zeros(o_ref.shape, o_ref.dtype)


def _moe_up(a, w1, w3, *, bm, bn, sub, tile_group, n_valid, tile_rows):
    M, K = a.shape
    E, _, N = w1.shape
    nm, nn, kh = M // bm, N // bn, K // 2
    assert nm * bm == M and nn * bn == N and kh * 2 == K and kh % LANES == 0
    tile = lambda s: jnp.maximum(s - 1, 0)

    def w_ahead(n, s, g, *_):
        wrap = s == nm
        return g[jnp.where(wrap, 0, jnp.minimum(s, nm - 1))], 0, jnp.where(wrap, jnp.minimum(n + 1, nn - 1), n)

    w_now = lambda n, s, g, *_: (g[tile(s)], 1, n)
    return pl.pallas_call(
        functools.partial(_moe_up_kernel, nm=nm, kh=kh, sub=sub),
        out_shape=jax.ShapeDtypeStruct((M, N), BF16),
        grid_spec=pltpu.PrefetchScalarGridSpec(
            num_scalar_prefetch=3,
            grid=(nn, nm + 1),
            in_specs=[pl.BlockSpec((bm, K), lambda n, s, *_: (tile(s), 0)),
                      pl.BlockSpec((1, kh, bn), w_ahead), pl.BlockSpec((1, kh, bn), w_ahead),
                      pl.BlockSpec((1, kh, bn), w_now), pl.BlockSpec((1, kh, bn), w_now)],
            out_specs=pl.BlockSpec((bm, bn), lambda n, s, *_: (tile(s), n)),
            scratch_shapes=[pltpu.VMEM((2, 2, kh, bn), BF16), pltpu.VMEM((2, kh, bn), BF16),
                            pltpu.SMEM((1,), jnp.int32)]),
        compiler_params=_cparams(("arbitrary", "arbitrary")),
        name="moe_up",
    )(tile_group, n_valid, tile_rows, a, w1, w3, w1, w3)


def _rms(x, g):
    return x * lax.rsqrt(jnp.mean(x * x, axis=-1, keepdims=True) + EPS) * g


def _adaln_kernel(x_ref, g_ref, sc_ref, sh_ref, o_ref):
    y = _rms(x_ref[...], g_ref[...])
    o_ref[...] = (y * (1.0 + sc_ref[0]) + sh_ref[0]).astype(o_ref.dtype)


def _adaln_stacked_kernel(xa_ref, xb_ref, g_ref, sc_ref, sh_ref, o_ref, *, n_first):
    x = jnp.where(pl.program_id(0) < n_first, xa_ref[...], xb_ref[...])
    y = _rms(x, g_ref[...])
    o_ref[...] = (y * (1.0 + sc_ref[0]) + sh_ref[0]).astype(o_ref.dtype)


def _adaln_stacked(xa, xb, g, scale, shift, *, group_rows, bt, out_dtype):
    Ta, D = xa.shape
    T = Ta + xb.shape[0]
    n_first = Ta // bt
    assert n_first * bt == Ta and xb.shape[0] % bt == 0
    mod_spec = pl.BlockSpec((1, 1, D), lambda i: ((i * bt) // group_rows, 0, 0))
    return pl.pallas_call(
        functools.partial(_adaln_stacked_kernel, n_first=n_first), grid=(T // bt,),
        in_specs=[pl.BlockSpec((bt, D), lambda i: (jnp.minimum(i, n_first - 1), 0)),
                  pl.BlockSpec((bt, D), lambda i: (jnp.maximum(i - n_first, 0), 0)),
                  pl.BlockSpec((1, D), lambda i: (0, 0)), mod_spec, mod_spec],
        out_specs=pl.BlockSpec((bt, D), lambda i: (i, 0)),
        out_shape=jax.ShapeDtypeStruct((T, D), out_dtype),
        compiler_params=_cparams(("parallel",)), name="adaln",
    )(xa, xb, g, scale, shift)


def _rmsnorm_kernel(x_ref, g_ref, o_ref):
    o_ref[...] = _rms(x_ref[...], g_ref[...]).astype(o_ref.dtype)


def _adaln_router_kernel(x_ref, g_ref, sc_ref, sh_ref, r_ref, o_ref, idx_ref, w_ref, *, n_experts):
    y = _rms(x_ref[...], g_ref[...])
    h = y * (1.0 + sc_ref[0]) + sh_ref[0]
    o_ref[...] = h
    logits = jnp.dot(h, r_ref[...], preferred_element_type=F32, precision=lax.Precision.HIGHEST)
    lane = lax.broadcasted_iota(jnp.int32, logits.shape, 1).astype(F32)
    lg = jnp.where(lane < n_experts, logits, -jnp.inf)
    v1 = jnp.max(lg, axis=-1, keepdims=True)
    i1 = jnp.min(jnp.where(lg == v1, lane, float(LANES)), axis=-1, keepdims=True)
    lg2 = jnp.where(lane == i1, -jnp.inf, lg)
    v2 = jnp.max(lg2, axis=-1, keepdims=True)
    i2 = jnp.min(jnp.where(lg2 == v2, lane, float(LANES)), axis=-1, keepdims=True)
    e = jnp.exp(v2 - v1)
    den = 1.0 + e
    idx_ref[...] = jnp.where(lane == 0, i1, jnp.where(lane == 1, i2, 0.0)).astype(jnp.int32)
    w_ref[...] = jnp.where(lane == 0, 1.0 / den, jnp.where(lane == 1, e / den, 0.0))


def _adaln(x, g, scale, shift, *, group_rows, bt, out_dtype, router=None):
    T, D = x.shape
    grid = (T // bt,)
    x_spec = pl.BlockSpec((bt, D), lambda i: (i, 0))
    g_spec = pl.BlockSpec((1, D), lambda i: (0, 0))
    mod_spec = pl.BlockSpec((1, 1, D), lambda i: ((i * bt) // group_rows, 0, 0))
    if router is None:
        return pl.pallas_call(
            _adaln_kernel, grid=grid, in_specs=[x_spec, g_spec, mod_spec, mod_spec], out_specs=x_spec,
            out_shape=jax.ShapeDtypeStruct((T, D), out_dtype),
            compiler_params=_cparams(("parallel",)), name="adaln",
        )(x, g, scale, shift)
    n_experts = router.shape[1]
    r_pad = jnp.pad(router, ((0, 0), (0, LANES - n_experts)))
    lane_spec = pl.BlockSpec((bt, LANES), lambda i: (i, 0))
    return pl.pallas_call(
        functools.partial(_adaln_router_kernel, n_experts=n_experts), grid=grid,
        in_specs=[x_spec, g_spec, mod_spec, mod_spec, pl.BlockSpec((D, LANES), lambda i: (0, 0))],
        out_specs=[x_spec, lane_spec, lane_spec],
        out_shape=[jax.ShapeDtypeStruct((T, D), F32), jax.ShapeDtypeStruct((T, LANES), jnp.int32),
                   jax.ShapeDtypeStruct((T, LANES), F32)],
        compiler_params=_cparams(("parallel",)), name="adaln_router",
    )(x, g, scale, shift, r_pad)


def _rmsnorm(x, g, *, bt, row0, n_rows):
    D = x.shape[1]
    r0 = row0 // bt
    assert r0 * bt == row0 and n_rows % bt == 0
    return pl.pallas_call(
        _rmsnorm_kernel, grid=(n_rows // bt,),
        in_specs=[pl.BlockSpec((bt, D), lambda i: (r0 + i, 0)), pl.BlockSpec((1, D), lambda i: (0, 0))],
        out_specs=pl.BlockSpec((bt, D), lambda i: (i, 0)), out_shape=jax.ShapeDtypeStruct((n_rows, D), F32),
        compiler_params=_cparams(("parallel",)), name="final_rmsnorm",
    )(x, g)


def _qk_prep_kernel(q0_ref, q1_ref, q2_ref, k_ref, cos_ref, sin_ref, qn_ref, kn_ref,
                    qo_ref, ko_ref, ks_ref):
    cos = cos_ref[...]
    sin = sin_ref[...]
    lane = lax.broadcasted_iota(jnp.int32, cos.shape, 1)
    first_half = (lane % (HEAD_DIM // 2)) < (HEAD_DIM // 4)

    def rope(y):
        partner = jnp.where(first_half, pltpu.roll(y, HEAD_DIM - HEAD_DIM // 4, 1),
                            pltpu.roll(y, HEAD_DIM // 4, 1))
        return y * cos + partner * sin

    q_heads_per_ref = q0_ref.shape[1] // HEAD_DIM
    for r, q_ref in enumerate((q0_ref, q1_ref, q2_ref)):
        for h in range(q_heads_per_ref):
            sl = slice(h * HEAD_DIM, (h + 1) * HEAD_DIM)
            y = _rms(q_ref[:, sl], qn_ref[...])
            col = (r * q_heads_per_ref + h) * HEAD_DIM
            qo_ref[:, col:col + HEAD_DIM] = rope(y).astype(qo_ref.dtype)
    for h in range(k_ref.shape[1] // HEAD_DIM):
        sl = slice(h * HEAD_DIM, (h + 1) * HEAD_DIM)
        y = _rms(k_ref[:, sl], kn_ref[...])
        ks_ref[:, sl] = y
        ko_ref[:, sl] = rope(y).astype(ko_ref.dtype)


def _qk_prep(u, cos, sin, q_norm, k_norm, *, pool_w, q_w, kv_w, bt):
    T = u.shape[0]
    assert q_w == 3 * kv_w and pool_w == kv_w
    cb = kv_w
    row = lambda i: (i, 0)
    in_specs = [pl.BlockSpec((bt, cb), lambda i, j=j: (i, j)) for j in (1, 2, 3, 4)]
    in_specs += [pl.BlockSpec((bt, HEAD_DIM), row), pl.BlockSpec((bt, HEAD_DIM), row),
                 pl.BlockSpec((1, HEAD_DIM), lambda i: (0, 0)), pl.BlockSpec((1, HEAD_DIM), lambda i: (0, 0))]
    return pl.pallas_call(
        _qk_prep_kernel, grid=(T // bt,), in_specs=in_specs,
        out_specs=[pl.BlockSpec((bt, q_w), row), pl.BlockSpec((bt, kv_w), row), pl.BlockSpec((bt, kv_w), row)],
        out_shape=[jax.ShapeDtypeStruct((T, q_w), BF16), jax.ShapeDtypeStruct((T, kv_w), BF16),
                   jax.ShapeDtypeStruct((T, kv_w), F32)],
        compiler_params=_cparams(("parallel",)), name="qk_norm_rope",
    )(u, u, u, u, cos, sin, q_norm, k_norm)


def _attn_kernel(q_ref, k_ref, v_ref, o_ref, *, heads, groups, scale):
    for h in range(heads):
        kv_sl = slice(h * HEAD_DIM, (h + 1) * HEAD_DIM)
        k = k_ref[:, kv_sl].astype(BF16)
        v = v_ref[:, kv_sl].astype(BF16)
        for g in range(groups):
            c = (h * groups + g) * HEAD_DIM
            q = q_ref[:, c:c + HEAD_DIM].astype(BF16)
            s = lax.dot_general(q, k, (((1,), (1,)), ((), ())), preferred_element_type=F32) * scale
            p = jnp.exp2(s - jnp.max(s, axis=-1, keepdims=True))
            l = jnp.sum(p, axis=-1, keepdims=True)
            o = jnp.dot(p.astype(BF16), v, preferred_element_type=F32) / l
            o_ref[:, c:c + HEAD_DIM] = o.astype(o_ref.dtype)


def _attention(q, k, v, *, batch, s_len, t_len, kv_heads, groups, bq, hp, q_row0=0, q_col0=0,
               k_row0=0, k_col0=0, v_col0=0):
    qw, kw = hp * groups * HEAD_DIM, hp * HEAD_DIM
    nq = s_len // bq
    assert q_row0 % bq == 0 and q_col0 % qw == 0 and k_row0 % t_len == 0 and kv_heads % hp == 0
    assert k_col0 % kw == 0 and v_col0 % kw == 0
    qr, qc, kr, kc, vc = q_row0 // bq, q_col0 // qw, k_row0 // t_len, k_col0 // kw, v_col0 // kw
    return pl.pallas_call(
        functools.partial(_attn_kernel, heads=hp, groups=groups, scale=HEAD_DIM ** -0.5 * LOG2E),
        grid=(batch, kv_heads // hp, nq),
        in_specs=[pl.BlockSpec((bq, qw), lambda b, h, i: (qr + b * nq + i, qc + h)),
                  pl.BlockSpec((t_len, kw), lambda b, h, i: (kr + b, kc + h)),
                  pl.BlockSpec((t_len, kw), lambda b, h, i: (kr + b, vc + h))],
        out_specs=pl.BlockSpec((bq, qw), lambda b, h, i: (b * nq + i, h)),
        out_shape=jax.ShapeDtypeStruct((batch * s_len, kv_heads * groups * HEAD_DIM), BF16),
        compiler_params=_cparams(("parallel", "parallel", "arbitrary")), name="attention",
    )(q, k, v)


def _pool_kernel(u_ref, w_ref, s_ref, o_ref, pad_ref, *, seq):
    g = pl.program_id(1)
    x = u_ref[...]
    pad_ref[0:POOL_HALO, :] = jnp.zeros((POOL_HALO, x.shape[1]), F32)
    pad_ref[POOL_HALO + seq:, :] = jnp.zeros((POOL_HALO, x.shape[1]), F32)
    pad_ref[POOL_HALO:POOL_HALO + seq, :] = x
    t = lax.broadcasted_iota(jnp.int32, x.shape, 0)
    for gi, win in enumerate(POOL_WINDOWS):
        @pl.when(g == gi)
        def _(win=win):
            back, fwd = win // 2, win - win // 2
            tot = pad_ref[POOL_HALO - back:POOL_HALO - back + seq, :]
            for j in range(-back + 1, fwd):
                tot = tot + pad_ref[POOL_HALO + j:POOL_HALO + j + seq, :]
            cnt = (jnp.minimum(t + fwd, seq) - jnp.maximum(t - back, 0)).astype(F32)
            diff = (tot / cnt - x).astype(BF16)
            y = jnp.dot(diff, w_ref[0], preferred_element_type=F32) * s_ref[...]
            o_ref[...] = y.astype(o_ref.dtype)


def _pool_mix(u, w_pool, pool_scale, *, n_seq, seq, row0):
    n_groups, cg, _ = w_pool.shape
    assert row0 % seq == 0 and max(POOL_WINDOWS) // 2 <= POOL_HALO and seq % 8 == 0
    r0 = row0 // seq
    return pl.pallas_call(
        functools.partial(_pool_kernel, seq=seq), grid=(n_seq, n_groups),
        in_specs=[pl.BlockSpec((seq, cg), lambda s, g: (r0 + s, g)),
                  pl.BlockSpec((1, cg, cg), lambda s, g: (g, 0, 0)),
                  pl.BlockSpec((1, cg), lambda s, g: (0, g))],
        out_specs=pl.BlockSpec((seq, cg), lambda s, g: (s, g)),
        out_shape=jax.ShapeDtypeStruct((n_seq * seq, n_groups * cg), BF16),
        scratch_shapes=[pltpu.VMEM((seq + 2 * POOL_HALO, cg), F32)],
        compiler_params=_cparams(("parallel", "arbitrary")), name="pool_mix",
    )(u, w_pool, pool_scale)


def _na_kernel(q_ref, k_ref, v_ref, kc_ref, vc_ref, t_ref, o_ref, bias_ref, *,
               rows_per_blk, key_rows, n_rows, wr, n_blk, scale):
    blk = pl.program_id(2)
    half = (key_rows - rows_per_blk) // 2
    W = GRID_W
    lane = lax.broadcasted_iota(jnp.int32, (W, 2 * W), 1)

    def build(b):
        k_start = min(max(b * rows_per_blk - half, 0), n_rows - key_rows)
        for qr in range(rows_per_blk):
            r = b * rows_per_blk + qr
            rs = min(max(r - wr // 2, 0), n_rows - wr)
            for kp in range(key_rows // 2):
                idx = [kk - r + NA_ROWS - 1 if rs <= kk < rs + wr else 2 * NA_ROWS - 1
                       for kk in (k_start + 2 * kp, k_start + 2 * kp + 1)]
                bias_ref[qr * W:(qr + 1) * W, kp * 2 * W:(kp + 1) * 2 * W] = jnp.where(
                    lane < W, t_ref[0, idx[0]], t_ref[0, idx[1]])

    pl.when(blk == 0)(functools.partial(build, 0))
    if n_blk > 2:
        pl.when(blk == 1)(functools.partial(build, 1))
    if n_blk > 1:
        pl.when(blk == n_blk - 1)(functools.partial(build, n_blk - 1))

    k_start = jnp.clip(blk * rows_per_blk - half, 0, n_rows - key_rows)
    tok0 = pl.multiple_of(k_start * W, W * 4)
    n_keys = key_rows * W
    q = q_ref[...].astype(BF16)
    kl = k_ref[pl.ds(tok0, n_keys), :].astype(BF16)
    vl = v_ref[pl.ds(tok0, n_keys), :].astype(BF16)
    kc = kc_ref[...].astype(BF16)
    vc = vc_ref[...].astype(BF16)
    dn = (((1,), (1,)), ((), ()))
    s_loc = lax.dot_general(q, kl, dn, preferred_element_type=F32) * scale + bias_ref[...]
    s_ctx = lax.dot_general(q, kc, dn, preferred_element_type=F32) * scale
    m = jnp.maximum(jnp.max(s_loc, axis=-1, keepdims=True), jnp.max(s_ctx, axis=-1, keepdims=True))
    p_loc = jnp.exp2(s_loc - m)
    p_ctx = jnp.exp2(s_ctx - m)
    l = jnp.sum(p_loc, axis=-1, keepdims=True) + jnp.sum(p_ctx, axis=-1, keepdims=True)
    o = (jnp.dot(p_ctx.astype(BF16), vc, preferred_element_type=F32)
         + jnp.dot(p_loc.astype(BF16), vl, preferred_element_type=F32)) / l
    o_ref[...] = o.astype(o_ref.dtype)


def _na_col_tables(rpb):
    H = rpb.shape[0]
    col = jnp.arange(GRID_W)
    cstart = jnp.clip(col - NA_COLS // 2, 0, GRID_W - NA_COLS)
    col_valid = (col[None, :] >= cstart[:, None]) & (col[None, :] < cstart[:, None] + NA_COLS)
    col_idx = jnp.clip(col[None, :] - col[:, None] + NA_COLS - 1, 0, 2 * NA_COLS - 2)
    onehot = (col_idx[None] == jnp.arange(2 * NA_COLS - 1)[:, None, None]).astype(F32)
    t = jnp.einsum("hrc,cqk->hrqk", rpb.astype(F32), onehot, precision=lax.Precision.HIGHEST)
    t = jnp.where(col_valid[None, None], t * LOG2E, NEG_INF)
    t = jnp.concatenate([t, jnp.full((H, 1, GRID_W, GRID_W), NEG_INF, F32)], axis=1)
    return jnp.concatenate([t, t], axis=-1)


def _na_attention(qkv, k_ctx, v_ctx, rpb, *, batch, n_tok, heads, row0, t_ctx):
    n_rows = n_tok // GRID_W
    wr = min(NA_ROWS, n_rows)
    rows_per_blk = min(NA_ROWS, n_rows)
    key_rows = min(rows_per_blk + wr, n_rows)
    n_blk = n_rows // rows_per_blk
    bq = rows_per_blk * GRID_W
    assert n_rows % rows_per_blk == 0 and row0 % n_tok == 0 and row0 % bq == 0 and key_rows % 2 == 0
    assert rows_per_blk % 4 == 0 and ((key_rows - rows_per_blk) // 2) % 4 == 0
    tables = _na_col_tables(rpb)
    qr0, kr0 = row0 // bq, row0 // n_tok
    return pl.pallas_call(
        functools.partial(_na_kernel, rows_per_blk=rows_per_blk, key_rows=key_rows, n_rows=n_rows, wr=wr,
                          n_blk=n_blk, scale=HEAD_DIM ** -0.5 * LOG2E),
        grid=(heads, batch, n_blk),
        in_specs=[pl.BlockSpec((bq, HEAD_DIM), lambda h, b, i: (qr0 + b * n_blk + i, h)),
                  pl.BlockSpec((n_tok, HEAD_DIM), lambda h, b, i: (kr0 + b, heads + h)),
                  pl.BlockSpec((n_tok, HEAD_DIM), lambda h, b, i: (kr0 + b, 2 * heads + h)),
                  pl.BlockSpec((t_ctx, HEAD_DIM), lambda h, b, i: (b, h)),
                  pl.BlockSpec((t_ctx, HEAD_DIM), lambda h, b, i: (b, h)),
                  pl.BlockSpec((1, 2 * NA_ROWS, GRID_W, 2 * GRID_W), lambda h, b, i: (h, 0, 0, 0))],
        out_specs=pl.BlockSpec((bq, HEAD_DIM), lambda h, b, i: (b * n_blk + i, h)),
        out_shape=jax.ShapeDtypeStruct((batch * n_tok, heads * HEAD_DIM), BF16),
        scratch_shapes=[pltpu.VMEM((bq, key_rows * GRID_W), F32)],
        compiler_params=_cparams(("arbitrary", "arbitrary", "arbitrary")), name="na_attention",
    )(qkv, qkv, qkv, k_ctx, v_ctx, tables)


def _row_copy(src_hbm, row, dst, slot, sem):
    return pltpu.make_async_copy(src_hbm.at[pl.ds(row, 1), :], dst.at[pl.ds(slot, 1), :], sem)


def _gather_kernel(src_ref, x_hbm, o_ref, buf, sem, *, rt, n_steps):
    i = pl.program_id(0)
    cur = i % 2

    def issue(step, s):
        def body(r, c):
            _row_copy(x_hbm, src_ref[step * rt + r], buf.at[s], r, sem.at[s]).start()
            return c
        lax.fori_loop(0, rt, body, 0, unroll=8)

    @pl.when(i == 0)
    def _():
        issue(0, 0)

    @pl.when(i + 1 < n_steps)
    def _():
        issue(i + 1, 1 - cur)

    pltpu.make_async_copy(x_hbm.at[pl.ds(0, rt), :], buf.at[cur], sem.at[cur]).wait()
    o_ref[...] = buf[cur].astype(o_ref.dtype)


def _gather_rows(x, src, *, rt):
    n = src.shape[0]
    D = x.shape[1]
    return pl.pallas_call(
        functools.partial(_gather_kernel, rt=rt, n_steps=n // rt),
        out_shape=jax.ShapeDtypeStruct((n, D), BF16),
        grid_spec=pltpu.PrefetchScalarGridSpec(
            num_scalar_prefetch=1, grid=(n // rt,),
            in_specs=[pl.BlockSpec(memory_space=pl.ANY)],
            out_specs=pl.BlockSpec((rt, D), lambda i, s: (i, 0)),
            scratch_shapes=[pltpu.VMEM((2, rt, D), F32), pltpu.SemaphoreType.DMA((2,))]),
        compiler_params=_cparams(("arbitrary",)), name="moe_dispatch",
    )(src, x)


def _combine_kernel(pos_ref, x_ref, g_ref, w_ref, y_hbm, o_ref, buf, sem, *, ct, n_steps):
    i = pl.program_id(0)
    cur = i % 2

    def issue(step, s):
        def body(r, c):
            for kk in range(TOP_K):
                _row_copy(y_hbm, pos_ref[TOP_K * (step * ct + r) + kk], buf.at[s, kk], r, sem.at[s]).start()
            return c
        lax.fori_loop(0, ct, body, 0, unroll=8)

    @pl.when(i == 0)
    def _():
        issue(0, 0)

    @pl.when(i + 1 < n_steps)
    def _():
        issue(i + 1, 1 - cur)

    for kk in range(TOP_K):
        pltpu.make_async_copy(y_hbm.at[pl.ds(0, ct), :], buf.at[cur, kk], sem.at[cur]).wait()
    w = w_ref[...]
    acc = w[:, 0:1] * buf[cur, 0]
    for kk in range(1, TOP_K):
        acc = acc + w[:, kk:kk + 1] * buf[cur, kk]
    o_ref[...] = x_ref[...] + g_ref[0] * acc


def _combine(x, gate, top_w, y, pos, *, ct, gate_rows):
    T, D = x.shape
    return pl.pallas_call(
        functools.partial(_combine_kernel, ct=ct, n_steps=T // ct),
        out_shape=jax.ShapeDtypeStruct((T, D), F32),
        grid_spec=pltpu.PrefetchScalarGridSpec(
            num_scalar_prefetch=1, grid=(T // ct,),
            in_specs=[pl.BlockSpec((ct, D), lambda i, p: (i, 0)),
                      pl.BlockSpec((1, 1, D), lambda i, p: ((i * ct) // gate_rows, 0, 0)),
                      pl.BlockSpec((ct, LANES), lambda i, p: (i, 0)),
                      pl.BlockSpec(memory_space=pl.ANY)],
            out_specs=pl.BlockSpec((ct, D), lambda i, p: (i, 0)),
            scratch_shapes=[pltpu.VMEM((2, TOP_K, ct, D), F32), pltpu.SemaphoreType.DMA((2,))]),
        compiler_params=_cparams(("arbitrary",)), name="moe_combine",
    )(pos, x, gate, top_w, y)


def _routing_plan(top_i, n_experts, tm):
    T = top_i.shape[0]
    e_flat = top_i.reshape(-1)
    onehot = (e_flat[:, None] == jnp.arange(n_experts)[None, :]).astype(jnp.int32)
    counts = jnp.sum(onehot, axis=0)
    rank = jnp.sum((jnp.cumsum(onehot, axis=0) - onehot) * onehot, axis=1)
    gsz = ((counts + tm - 1) // tm) * tm
    gend = jnp.cumsum(gsz)
    goff = gend - gsz
    pos = (goff[e_flat] + rank).astype(jnp.int32)
    n_tiles = (T * TOP_K) // tm + n_experts
    src = jnp.zeros((n_tiles * tm,), jnp.int32).at[pos].set(jnp.arange(T * TOP_K, dtype=jnp.int32) // TOP_K)
    n_valid = (gend[-1] // tm).astype(jnp.int32)
    tile_row = jnp.minimum(jnp.arange(n_tiles, dtype=jnp.int32), n_valid - 1) * tm
    tile_expert = jnp.sum((tile_row[:, None] >= gend[None, :]).astype(jnp.int32), axis=1)
    tile_rows = jnp.clip((goff + counts)[tile_expert] - tile_row, 0, tm)
    return pos, src, tile_expert.astype(jnp.int32), n_valid.reshape(1), tile_rows.astype(jnp.int32)


def _pick(n, prefs):
    for p in prefs:
        if n % p == 0:
            return p
    return n


def _modulation(cvecs, w, b):
    n_c, D = cvecs.shape
    rows = 8
    a = jnp.zeros((rows, D), BF16).at[:n_c].set(jax.nn.silu(cvecs).astype(BF16))
    m = _matmul(a, (w[None],), bm=rows, bn=_pick(6 * D, (1024, 512, 256, 128)), bk=D,
                mode="bias", extra=(b[None, :],), name="modulation")
    return jnp.transpose(m[:n_c].reshape(n_c, 6, 1, D), (1, 0, 2, 3))


def _rope_tables(n_prompt_rows, dec_batch, dec_seq):
    t = jnp.arange(dec_seq)
    pos = jnp.stack([t // GRID_W, t % GRID_W], axis=-1).astype(F32)
    n_freq = HEAD_DIM // 4
    inv = ROPE_THETA ** (-jnp.arange(n_freq, dtype=F32) / n_freq)
    ang = pos[:, :, None] * inv
    cos = jnp.concatenate([jnp.cos(ang)] * 2, axis=-1).reshape(dec_seq, HEAD_DIM)
    sin = jnp.concatenate([-jnp.sin(ang), jnp.sin(ang)], axis=-1).reshape(dec_seq, HEAD_DIM)
    cos = jnp.concatenate([jnp.ones((n_prompt_rows, HEAD_DIM), F32)] + [cos] * dec_batch)
    sin = jnp.concatenate([jnp.zeros((n_prompt_rows, HEAD_DIM), F32)] + [sin] * dec_batch)
    return cos, sin


def kernel(x_prompt, x_sample, c, c_ctx, cache_l0_attn_k, cache_l0_attn_v, cache_l1_na_k, cache_l1_na_v,
           l0_ada_w, l0_ada_b, l0_norm1, l0_w_in, l0_q_norm, l0_k_norm, l0_w_pool, l0_pool_scale, l0_w_out,
           l0_norm2, l0_ffn_w1, l0_ffn_w3, l0_ffn_w2,
           l1_ada_w, l1_ada_b, l1_norm1, l1_w_qkv, l1_rpb, l1_w_out, l1_norm2, l1_router,
           l1_exp_w1, l1_exp_w3, l1_exp_w2, final_norm):
    batch, seq, D = x_prompt.shape
    dec_batch, dec_seq, _ = x_sample.shape
    past_len = cache_l0_attn_k.shape[1]
    kv_heads0 = cache_l0_attn_k.shape[2]
    heads1 = cache_l1_na_k.shape[2]
    n_pool, cg, _ = l0_w_pool.shape
    pool_w = n_pool * cg
    kv_w = kv_heads0 * HEAD_DIM
    q_w = l0_w_in.shape[1] - pool_w - 2 * kv_w
    groups0 = q_w // kv_w
    c_w = heads1 * HEAD_DIM
    n_experts = l1_router.shape[1]
    Tp, Ts = batch * seq, dec_batch * dec_seq
    T = Tp + Ts
    group_rows = math.gcd(Tp, dec_seq)
    grp_to_c = jnp.array([0] * (Tp // group_rows) + [1 + b for b in range(dec_batch)
                                                      for _ in range(dec_seq // group_rows)], jnp.int32)

    bt = _pick(group_rows, (256, 128, 64, 32, 16, 8))
    bm = _pick(group_rows, (1024, 512, 256, 128, 64, 32, 16, 8))
    bm2 = _pick(group_rows, (512, 256, 128, 64, 32, 16, 8))
    blk = lambda n: _pick(n, (512, 256, 128))
    seq_blk = lambda n: _pick(n, (256, 128, 64, 32, 16, 8))
    mods = lambda m: [m[i][grp_to_c] for i in range(6)]
    cvecs = jnp.concatenate([c_ctx[None, :], c], axis=0)
    gg = dict(gate_rows=group_rows)

    xa, xb = x_prompt.reshape(Tp, D), x_sample.reshape(Ts, D)

    sh1, sc1, g1, sh2, sc2, g2 = mods(_modulation(cvecs, l0_ada_w, l0_ada_b))
    h = _adaln_stacked(xa, xb, l0_norm1[None], sc1, sh1, group_rows=group_rows, bt=bt, out_dtype=BF16)
    bn_in = blk(l0_w_in.shape[1])
    u, ffn_w2 = _matmul_with_cast(h, (l0_w_in[None],), l0_ffn_w2, bm=bm, bn=bn_in, bk=D, name="l0_in_proj")
    cos, sin = _rope_tables(Tp, dec_batch, dec_seq)
    q_att, k_att, k_state = _qk_prep(u, cos, sin, l0_q_norm[None], l0_k_norm[None],
                                     pool_w=pool_w, q_w=q_w, kv_w=kv_w, bt=bt)
    v_col0 = pool_w + q_w + kv_w
    att_p = _attention(q_att, k_att, u, batch=batch, s_len=seq, t_len=seq, kv_heads=kv_heads0, groups=groups0,
                       bq=seq_blk(seq), hp=kv_heads0, v_col0=v_col0)
    k_lat = jnp.concatenate([cache_l0_attn_k.reshape(dec_batch, past_len, kv_w).astype(BF16),
                             k_att[Tp:].reshape(dec_batch, dec_seq, kv_w)], axis=1).reshape(-1, kv_w)
    v_lat = jnp.concatenate([cache_l0_attn_v.reshape(dec_batch, past_len, kv_w),
                             u[Tp:, v_col0:].reshape(dec_batch, dec_seq, kv_w)], axis=1).reshape(-1, kv_w)
    att_s = _attention(q_att, k_lat, v_lat, batch=dec_batch, s_len=dec_seq, t_len=past_len + dec_seq,
                       kv_heads=kv_heads0, groups=groups0, bq=seq_blk(dec_seq), hp=1, q_row0=Tp)
    w_pool = l0_w_pool.astype(BF16)
    pool_p = _pool_mix(u, w_pool, l0_pool_scale[None], n_seq=batch, seq=seq, row0=0)
    pool_s = _pool_mix(u, w_pool, l0_pool_scale[None], n_seq=dec_batch, seq=dec_seq, row0=Tp)
    x = _matmul([(pool_p, pool_s), (att_p, att_s)], (l0_w_out[None],), bm=bm2, bn=blk(D), bk=pool_w + q_w,
                mode="gated_res", extra=(xa, xb, g1), n_first=Tp // bm2, name="l0_out_proj", **gg)
    h = _adaln(x, l0_norm2[None], sc2, sh2, group_rows=group_rows, bt=bt, out_dtype=BF16)
    d_ff = l0_ffn_w1.shape[1]
    d_fe = l1_exp_w1.shape[2]
    hh, exp_w2 = _matmul_with_cast(h, (l0_ffn_w1[None], l0_ffn_w3[None]), l1_exp_w2.reshape(n_experts * d_fe, D),
                                   bm=bm, bn=blk(d_ff), bk=D, mode="swiglu", out_dtype=BF16, name="l0_ffn_up")
    x = _matmul(hh, (ffn_w2[None],), bm=bm, bn=blk(D),
                bk=_pick(d_ff, (5504, 2048, 1024, 512, 256, 128)),
                mode="gated_res", extra=(x, g2), name="l0_ffn_down", **gg)
    state_l0_k = k_state[:Tp].reshape(batch, seq, kv_heads0, HEAD_DIM)
    state_l0_v = u[:Tp, v_col0:].reshape(batch, seq, kv_heads0, HEAD_DIM)

    sh1, sc1, g1, sh2, sc2, g2 = mods(_modulation(cvecs, l1_ada_w, l1_ada_b))
    h = _adaln(x, l1_norm1[None], sc1, sh1, group_rows=group_rows, bt=bt, out_dtype=BF16)
    qkv = _matmul(h, (l1_w_qkv[None],), bm=bm, bn=blk(3 * c_w), bk=D, name="l1_qkv_proj")
    att_p = _attention(qkv, qkv, qkv, batch=batch, s_len=seq, t_len=seq, kv_heads=heads1, groups=1,
                       bq=seq_blk(seq), hp=_pick(heads1, (8, 4, 2, 1)), k_col0=c_w, v_col0=2 * c_w)
    att_s = _na_attention(qkv, cache_l1_na_k.reshape(dec_batch * past_len, c_w),
                          cache_l1_na_v.reshape(dec_batch * past_len, c_w), l1_rpb,
                          batch=dec_batch, n_tok=dec_seq, heads=heads1, row0=Tp, t_ctx=past_len)
    x = _matmul([(att_p, att_s)], (l1_w_out[None],), bm=bm2, bn=blk(D), bk=c_w,
                mode="gated_res", extra=(x, g1), n_first=Tp // bm2, name="l1_out_proj", **gg)
    h32, top_i, top_w = _adaln(x, l1_norm2[None], sc2, sh2, group_rows=group_rows, bt=bt, out_dtype=F32,
                               router=l1_router)
    tm = _pick(T * TOP_K, (512, 256, 128, 64, 32, 16, 8))
    sub = LANES if tm % LANES == 0 else tm
    pos, src, tile_expert, n_valid, tile_rows = _routing_plan(top_i[:, :TOP_K], n_experts, tm)
    routed = dict(tile_group=tile_expert, n_valid=n_valid, tile_rows=tile_rows, sub=sub)
    xs = _gather_rows(h32, src, rt=_pick(tm, (256, 128, 64, 32, 16, 8)))
    he = _moe_up(xs, l1_exp_w1, l1_exp_w3, bm=tm, bn=blk(d_fe), **routed)
    ye = _matmul(he, (exp_w2.reshape(n_experts, d_fe, D),), bm=tm, bn=_pick(D, (1024, 512, 256, 128)),
                 bk=_pick(d_fe, (7168, 3584, 2048, 1024, 512, 256, 128)), name="moe_down", **routed)
    x = _combine(x, g2, top_w, ye, pos, ct=_pick(group_rows, (128, 64, 32, 16, 8)), gate_rows=group_rows)
    state_l1_k = qkv[:Tp, c_w:2 * c_w].reshape(batch, seq, heads1, HEAD_DIM)
    state_l1_v = qkv[:Tp, 2 * c_w:].reshape(batch, seq, heads1, HEAD_DIM)

    y_p = _rmsnorm(x, final_norm[None], bt=bt, row0=0, n_rows=Tp)
    y_s = _rmsnorm(x, final_norm[None], bt=bt, row0=Tp, n_rows=Ts)
    return (y_p.reshape(batch, seq, D), y_s.reshape(dec_batch, dec_seq, D),
            state_l0_k, state_l0_v, state_l1_k, state_l1_v)
```

```python
import functools
import math

import jax
import jax.numpy as jnp
from jax import lax
from jax.experimental import pallas as pl
from jax.experimental.pallas import tpu as pltpu

HEAD_DIM = 128
GRID_W = 64
POOL_WINDOWS = (2, 4, 8, 16)
NA_ROWS = 8
NA_COLS = 16
TOP_K = 2
EPS = 1e-6
ROPE_THETA = 10000.0
NEG_INF = -1e30
LOG2E = math.log2(math.e)
LANES = 128
POOL_HALO = 16
VMEM_LIMIT_BYTES = 60 * 1024 * 1024

F32 = jnp.float32
BF16 = jnp.bfloat16


def _cparams(sem):
    return pltpu.CompilerParams(dimension_semantics=sem, vmem_limit_bytes=VMEM_LIMIT_BYTES)


def _partial_tile(rows, bm, sub, k_last, compute, o_ref):
    nearly_full = rows > bm - sub

    @pl.when(nearly_full)
    def _():
        compute(slice(0, bm))

    @pl.when(jnp.logical_not(nearly_full))
    def _():
        for j in range(bm // sub):
            c = slice(j * sub, (j + 1) * sub)
            if j < bm // sub - 1:
                pl.when(rows > j * sub)(functools.partial(compute, c))

            @pl.when(jnp.logical_and(rows <= j * sub, k_last))
            def _(c=c):
                o_ref[c, :] = jnp.zeros((sub, o_ref.shape[1]), o_ref.dtype)


def _mm_kernel(grp_ref, nvalid_ref, rows_ref, *refs, nk, mode, cast_w, sub, n_first, seg_stacked, res_stacked,
               side):
    n_w = 2 if mode == "swiglu" else 1
    n_a = sum(2 if st else 1 for st in seg_stacked)
    a_refs = refs[:n_a]
    w_refs = refs[n_a:n_a + n_w]
    n_extra = {"plain": 0, "swiglu": 0, "bias": 1, "gated_res": 3 if res_stacked else 2}[mode]
    pos = n_a + n_w
    extra = refs[pos:pos + n_extra]
    pos += n_extra
    if side:
        side_in, o_ref, side_out = refs[pos], refs[pos + 1], refs[pos + 2]
        pos += 3
    else:
        o_ref = refs[pos]
        pos += 1
    scratch = refs[pos:]
    acc_in_out = nk > 1 and not scratch
    bm = a_refs[0].shape[0]
    m = pl.program_id(1)
    k = pl.program_id(2)

    def a_segments(rs):
        out, i = [], 0
        for st in seg_stacked:
            if st:
                out.append(jnp.where(m < n_first, a_refs[i][rs, :], a_refs[i + 1][rs, :]))
                i += 2
            else:
                out.append(a_refs[i][rs, :])
                i += 1
        return out

    def epilogue(accs, rs):
        if mode == "plain":
            r = accs[0]
        elif mode == "swiglu":
            g = accs[0]
            r = (g / (1.0 + jnp.exp(-g))) * accs[1]
        elif mode == "bias":
            r = accs[0] + extra[0][...]
        else:
            res = jnp.where(m < n_first, extra[0][rs, :], extra[1][rs, :]) if res_stacked else extra[0][rs, :]
            r = res + extra[-1][0] * accs[0]
        o_ref[rs, :] = r.astype(o_ref.dtype)

    def compute(rs):
        w_of = (lambda j: scratch[j]) if cast_w else (lambda j: w_refs[j].at[0])
        segs = a_segments(rs)
        parts = []
        for j in range(n_w):
            acc, off = None, 0
            for a in segs:
                d = jnp.dot(a, w_of(j)[off:off + a.shape[1], :], preferred_element_type=F32)
                acc = d if acc is None else acc + d
                off += a.shape[1]
            parts.append(acc)
        if side:
            side_out[...] = side_in[...].astype(BF16)
        if nk == 1:
            epilogue(parts, rs)
        elif acc_in_out:
            @pl.when(k == 0)
            def _():
                o_ref[rs, :] = parts[0]

            @pl.when(k > 0)
            def _():
                o_ref[rs, :] += parts[0]
        else:
            @pl.when(k == 0)
            def _():
                for acc, p in zip(scratch, parts):
                    acc[rs, :] = p

            @pl.when(k > 0)
            def _():
                for acc, p in zip(scratch, parts):
                    acc[rs, :] += p

            @pl.when(k == nk - 1)
            def _():
                epilogue([acc[rs, :] for acc in scratch], rs)

    @pl.when(m < nvalid_ref[0])
    def _():
        if cast_w:
            @pl.when(jnp.logical_or(m == 0, grp_ref[m] != grp_ref[jnp.maximum(m - 1, 0)]))
            def _():
                for w_ref, wb in zip(w_refs, scratch):
                    wb[...] = w_ref[0].astype(BF16)
        if sub is None:
            compute(slice(0, bm))
        else:
            _partial_tile(rows_ref[m], bm, sub, k == nk - 1, compute, o_ref)

    @pl.when(jnp.logical_and(m >= nvalid_ref[0], k == nk - 1))
    def _():
        o_ref[...] = jnp.zeros(o_ref.shape, o_ref.dtype)


def _matmul(a, ws, *, bm, bn, bk, mode="plain", out_dtype=F32, extra=(), tile_group=None,
            n_valid=None, tile_rows=None, sub=None, gate_rows=None, n_first=None, side=None, name="mm"):
    segs = a if isinstance(a, (list, tuple)) else [a]
    seg_stacked = tuple(isinstance(sg, tuple) for sg in segs)
    seg_rows = lambda sg: sum(p.shape[0] for p in sg) if isinstance(sg, tuple) else sg.shape[0]
    seg_cols = lambda sg: sg[0].shape[1] if isinstance(sg, tuple) else sg.shape[1]
    M, K = seg_rows(segs[0]), sum(seg_cols(sg) for sg in segs)
    E, _, N = ws[0].shape
    nm, nn, nk = M // bm, N // bn, K // bk
    assert nm * bm == M and nn * bn == N and nk * bk == K, (M, K, ws[0].shape, bm, bn, bk)
    assert all(seg_rows(sg) == M for sg in segs) and (nk == 1 or len(segs) == 1 and not seg_stacked[0])
    cast_w = ws[0].dtype == F32
    assert not (cast_w and nk > 1) and not (side is not None and (sub is not None or nk > 1))
    if tile_group is None:
        tile_group = jnp.zeros((nm,), jnp.int32)
        n_valid = jnp.full((1,), nm, jnp.int32)
    if tile_rows is None:
        tile_rows = jnp.full((nm,), bm, jnp.int32)
    kk = lambda m, k: jnp.where(m % 2 == 1, nk - 1 - k, k) if nk > 1 else k
    first_rows = lambda n, m, k, *_: (jnp.minimum(m, n_first - 1), 0)
    second_rows = lambda n, m, k, *_: (jnp.maximum(m - n_first, 0), 0)
    in_specs, a_args = [], []
    for sg in segs:
        if isinstance(sg, tuple):
            assert sg[0].shape[0] == n_first * bm
            in_specs += [pl.BlockSpec((bm, sg[0].shape[1]), first_rows), pl.BlockSpec((bm, sg[1].shape[1]), second_rows)]
            a_args += list(sg)
        else:
            in_specs.append(pl.BlockSpec((bm, bk if nk > 1 else sg.shape[1]), lambda n, m, k, *_: (m, kk(m, k))))
            a_args.append(sg)
    for _ in ws:
        in_specs.append(pl.BlockSpec((1, bk, bn), lambda n, m, k, g, *_: (g[m], kk(m, k), n)))
    if mode == "bias":
        in_specs.append(pl.BlockSpec((1, bn), lambda n, m, k, *_: (0, n)))
    elif mode == "gated_res":
        if len(extra) == 2:
            in_specs.append(pl.BlockSpec((bm, bn), lambda n, m, k, *_: (m, n)))
        else:
            in_specs.append(pl.BlockSpec((bm, bn), lambda n, m, k, *_: (jnp.minimum(m, n_first - 1), n)))
            in_specs.append(pl.BlockSpec((bm, bn), lambda n, m, k, *_: (jnp.maximum(m - n_first, 0), n)))
        in_specs.append(pl.BlockSpec((1, 1, bn), lambda n, m, k, *_: ((m * bm) // gate_rows, 0, n)))
    if cast_w:
        scratch = [pltpu.VMEM((bk, bn), BF16) for _ in ws]
    elif nk == 1 or (mode == "plain" and out_dtype == F32):
        scratch = []
    else:
        scratch = [pltpu.VMEM((bm, bn), F32) for _ in ws]
    out_shape = jax.ShapeDtypeStruct((M, N), out_dtype)
    out_specs = pl.BlockSpec((bm, bn), lambda n, m, k, *_: (m, n))
    side_args = ()
    if side is not None:
        src, rows = side
        n_side = src.shape[0] // rows
        assert n_side * rows == src.shape[0] and n_side <= nn * nm
        side_spec = pl.BlockSpec((rows, src.shape[1]), lambda n, m, k, *_: (jnp.minimum(n * nm + m, n_side - 1), 0))
        in_specs.append(side_spec)
        out_shape = [out_shape, jax.ShapeDtypeStruct(src.shape, BF16)]
        out_specs = [out_specs, side_spec]
        side_args = (src,)
    return pl.pallas_call(
        functools.partial(_mm_kernel, nk=nk, mode=mode, cast_w=cast_w, sub=sub, n_first=n_first,
                          seg_stacked=seg_stacked, res_stacked=mode == "gated_res" and len(extra) == 3,
                          side=side is not None),
        out_shape=out_shape,
        grid_spec=pltpu.PrefetchScalarGridSpec(
            num_scalar_prefetch=3,
            grid=(nn, nm, nk),
            in_specs=in_specs,
            out_specs=out_specs,
            scratch_shapes=scratch),
        compiler_params=_cparams(("arbitrary", "arbitrary", "arbitrary")),
        name=name,
    )(tile_group, n_valid, tile_rows, *a_args, *ws, *extra, *side_args)


def _matmul_with_cast(a, ws, src, *, bm, bn, **kw):
    steps = (a.shape[0] // bm) * (ws[0].shape[2] // bn)
    R = src.shape[0]
    rows = next((r for r in range(16, R + 1, 16) if R % r == 0 and R // r <= steps), None)
    if rows is None:
        return _matmul(a, ws, bm=bm, bn=bn, **kw), src.astype(BF16)
    out, cast = _matmul(a, ws, bm=bm, bn=bn, side=(src, rows), **kw)
    return out, cast


def _moe_up_kernel(grp_ref, nvalid_ref, rows_ref, a_ref, w1a_ref, w3a_ref, w1b_ref, w3b_ref, o_ref,
                   wa_ref, wb_ref, par_ref, *, nm, kh, sub):
    n = pl.program_id(0)
    s = pl.program_id(1)
    m = s - 1
    first = jnp.logical_and(n == 0, s == 0)
    g_cur = grp_ref[jnp.clip(m, 0, nm - 1)]
    g_prev = grp_ref[jnp.clip(m - 1, 0, nm - 1)]
    g_next = grp_ref[jnp.clip(m + 1, 0, nm - 1)]

    @pl.when(first)
    def _():
        par_ref[0] = 0

    @pl.when(jnp.logical_or(s == 1, jnp.logical_and(s > 1, g_cur != g_prev)))
    def _():
        par_ref[0] = 1 - par_ref[0]
        wb_ref[0] = w1b_ref[0].astype(BF16)
        wb_ref[1] = w3b_ref[0].astype(BF16)

    p = par_ref[0]

    nxt_new = jnp.logical_or(jnp.logical_or(first, s == nm),
                             jnp.logical_and(jnp.logical_and(s >= 1, s < nm), g_next != g_cur))

    @pl.when(nxt_new)
    def _():
        wa_ref[1 - p, 0] = w1a_ref[0].astype(BF16)
        wa_ref[1 - p, 1] = w3a_ref[0].astype(BF16)

    def compute(rs):
        a_lo = a_ref[rs, :kh]
        a_hi = a_ref[rs, kh:]
        g = (jnp.dot(a_lo, wa_ref[p, 0], preferred_element_type=F32)
             + jnp.dot(a_hi, wb_ref[0], preferred_element_type=F32))
        u = (jnp.dot(a_lo, wa_ref[p, 1], preferred_element_type=F32)
             + jnp.dot(a_hi, wb_ref[1], preferred_element_type=F32))
        o_ref[rs, :] = ((g / (1.0 + jnp.exp(-g))) * u).astype(o_ref.dtype)

    @pl.when(jnp.logical_and(s >= 1, m < nvalid_ref[0]))
    def _():
        _partial_tile(rows_ref[jnp.clip(m, 0, nm - 1)], a_ref.shape[0], sub, True, compute, o_ref)

    @pl.when(jnp.logical_and(s >= 1, m >= nvalid_ref[0]))
    def _():
        o_ref[...] = jnp.zeros(o_ref.shape, o_ref.dtype)


def _moe_up(a, w1, w3, *, bm, bn, sub, tile_group, n_valid, tile_rows):
    M, K = a.shape
    E, _, N = w1.shape
    nm, nn, kh = M // bm, N // bn, K // 2
    assert nm * bm == M and nn * bn == N and kh * 2 == K and kh % LANES == 0
    tile = lambda s: jnp.maximum(s - 1, 0)

    def w_ahead(n, s, g, *_):
        wrap = s == nm
        return g[jnp.where(wrap, 0, jnp.minimum(s, nm - 1))], 0, jnp.where(wrap, jnp.minimum(n + 1, nn - 1), n)

    w_now = lambda n, s, g, *_: (g[tile(s)], 1, n)
    return pl.pallas_call(
        functools.partial(_moe_up_kernel, nm=nm, kh=kh, sub=sub),
        out_shape=jax.ShapeDtypeStruct((M, N), BF16),
        grid_spec=pltpu.PrefetchScalarGridSpec(
            num_scalar_prefetch=3,
            grid=(nn, nm + 1),
            in_specs=[pl.BlockSpec((bm, K), lambda n, s, *_: (tile(s), 0)),
                      pl.BlockSpec((1, kh, bn), w_ahead), pl.BlockSpec((1, kh, bn), w_ahead),
                      pl.BlockSpec((1, kh, bn), w_now), pl.BlockSpec((1, kh, bn), w_now)],
            out_specs=pl.BlockSpec((bm, bn), lambda n, s, *_: (tile(s), n)),
            scratch_shapes=[pltpu.VMEM((2, 2, kh, bn), BF16), pltpu.VMEM((2, kh, bn), BF16),
                            pltpu.SMEM((1,), jnp.int32)]),
        compiler_params=_cparams(("arbitrary", "arbitrary")),
        name="moe_up",
    )(tile_group, n_valid, tile_rows, a, w1, w3, w1, w3)


def _rms(x, g):
    return x * lax.rsqrt(jnp.mean(x * x, axis=-1, keepdims=True) + EPS) * g


def _adaln_kernel(x_ref, g_ref, sc_ref, sh_ref, o_ref):
    y = _rms(x_ref[...], g_ref[...])
    o_ref[...] = (y * (1.0 + sc_ref[0]) + sh_ref[0]).astype(o_ref.dtype)


def _adaln_stacked_kernel(xa_ref, xb_ref, g_ref, sc_ref, sh_ref, o_ref, *, n_first):
    x = jnp.where(pl.program_id(0) < n_first, xa_ref[...], xb_ref[...])
    y = _rms(x, g_ref[...])
    o_ref[...] = (y * (1.0 + sc_ref[0]) + sh_ref[0]).astype(o_ref.dtype)


def _adaln_stacked(xa, xb, g, scale, shift, *, group_rows, bt, out_dtype):
    Ta, D = xa.shape
    T = Ta + xb.shape[0]
    n_first = Ta // bt
    assert n_first * bt == Ta and xb.shape[0] % bt == 0
    mod_spec = pl.BlockSpec((1, 1, D), lambda i: ((i * bt) // group_rows, 0, 0))
    return pl.pallas_call(
        functools.partial(_adaln_stacked_kernel, n_first=n_first), grid=(T // bt,),
        in_specs=[pl.BlockSpec((bt, D), lambda i: (jnp.minimum(i, n_first - 1), 0)),
                  pl.BlockSpec((bt, D), lambda i: (jnp.maximum(i - n_first, 0), 0)),
                  pl.BlockSpec((1, D), lambda i: (0, 0)), mod_spec, mod_spec],
        out_specs=pl.BlockSpec((bt, D), lambda i: (i, 0)),
        out_shape=jax.ShapeDtypeStruct((T, D), out_dtype),
        compiler_params=_cparams(("parallel",)), name="adaln",
    )(xa, xb, g, scale, shift)


def _adaln_router_kernel(x_ref, g_ref, sc_ref, sh_ref, r_ref, o_ref, idx_ref, w_ref, *, n_experts):
    y = _rms(x_ref[...], g_ref[...])
    h = y * (1.0 + sc_ref[0]) + sh_ref[0]
    o_ref[...] = h
    logits = jnp.dot(h, r_ref[...], preferred_element_type=F32, precision=lax.Precision.HIGHEST)
    lane = lax.broadcasted_iota(jnp.int32, logits.shape, 1).astype(F32)
    lg = jnp.where(lane < n_experts, logits, -jnp.inf)
    v1 = jnp.max(lg, axis=-1, keepdims=True)
    i1 = jnp.min(jnp.where(lg == v1, lane, float(LANES)), axis=-1, keepdims=True)
    lg2 = jnp.where(lane == i1, -jnp.inf, lg)
    v2 = jnp.max(lg2, axis=-1, keepdims=True)
    i2 = jnp.min(jnp.where(lg2 == v2, lane, float(LANES)), axis=-1, keepdims=True)
    e = jnp.exp(v2 - v1)
    den = 1.0 + e
    idx_ref[...] = jnp.where(lane == 0, i1, jnp.where(lane == 1, i2, 0.0)).astype(jnp.int32)
    w_ref[...] = jnp.where(lane == 0, 1.0 / den, jnp.where(lane == 1, e / den, 0.0))


def _adaln(x, g, scale, shift, *, group_rows, bt, out_dtype, router=None):
    T, D = x.shape
    grid = (T // bt,)
    x_spec = pl.BlockSpec((bt, D), lambda i: (i, 0))
    g_spec = pl.BlockSpec((1, D), lambda i: (0, 0))
    mod_spec = pl.BlockSpec((1, 1, D), lambda i: ((i * bt) // group_rows, 0, 0))
    if router is None:
        return pl.pallas_call(
            _adaln_kernel, grid=grid, in_specs=[x_spec, g_spec, mod_spec, mod_spec], out_specs=x_spec,
            out_shape=jax.ShapeDtypeStruct((T, D), out_dtype),
            compiler_params=_cparams(("parallel",)), name="adaln",
        )(x, g, scale, shift)
    n_experts = router.shape[1]
    r_pad = jnp.pad(router, ((0, 0), (0, LANES - n_experts)))
    lane_spec = pl.BlockSpec((bt, LANES), lambda i: (i, 0))
    return pl.pallas_call(
        functools.partial(_adaln_router_kernel, n_experts=n_experts), grid=grid,
        in_specs=[x_spec, g_spec, mod_spec, mod_spec, pl.BlockSpec((D, LANES), lambda i: (0, 0))],
        out_specs=[x_spec, lane_spec, lane_spec],
        out_shape=[jax.ShapeDtypeStruct((T, D), F32), jax.ShapeDtypeStruct((T, LANES), jnp.int32),
                   jax.ShapeDtypeStruct((T, LANES), F32)],
        compiler_params=_cparams(("parallel",)), name="adaln_router",
    )(x, g, scale, shift, r_pad)


def _qk_prep_kernel(q0_ref, q1_ref, q2_ref, k_ref, cos_ref, sin_ref, qn_ref, kn_ref,
                    qo_ref, ko_ref, ks_ref, *, n_plain):
    cos = cos_ref[...]
    sin = sin_ref[...]
    lane = lax.broadcasted_iota(jnp.int32, cos.shape, 1)
    first_half = (lane % (HEAD_DIM // 2)) < (HEAD_DIM // 4)

    def rope(y):
        partner = jnp.where(first_half, pltpu.roll(y, HEAD_DIM - HEAD_DIM // 4, 1),
                            pltpu.roll(y, HEAD_DIM // 4, 1))
        return y * cos + partner * sin

    def emit(pos_fn):
        q_heads_per_ref = q0_ref.shape[1] // HEAD_DIM
        for r, q_ref in enumerate((q0_ref, q1_ref, q2_ref)):
            for h in range(q_heads_per_ref):
                sl = slice(h * HEAD_DIM, (h + 1) * HEAD_DIM)
                y = _rms(q_ref[:, sl], qn_ref[...])
                col = (r * q_heads_per_ref + h) * HEAD_DIM
                qo_ref[:, col:col + HEAD_DIM] = pos_fn(y).astype(qo_ref.dtype)
        for h in range(k_ref.shape[1] // HEAD_DIM):
            sl = slice(h * HEAD_DIM, (h + 1) * HEAD_DIM)
            y = _rms(k_ref[:, sl], kn_ref[...])
            ks_ref[:, sl] = y
            ko_ref[:, sl] = pos_fn(y).astype(ko_ref.dtype)

    pl.when(pl.program_id(0) < n_plain)(functools.partial(emit, lambda y: y))
    pl.when(pl.program_id(0) >= n_plain)(functools.partial(emit, rope))


def _qk_prep(u, cos, sin, q_norm, k_norm, *, pool_w, q_w, kv_w, bt, n_plain_rows):
    T = u.shape[0]
    assert q_w == 3 * kv_w and pool_w == kv_w and n_plain_rows % bt == 0
    cb = kv_w
    row = lambda i: (i, 0)
    in_specs = [pl.BlockSpec((bt, cb), lambda i, j=j: (i, j)) for j in (1, 2, 3, 4)]
    in_specs += [pl.BlockSpec((bt, HEAD_DIM), row), pl.BlockSpec((bt, HEAD_DIM), row),
                 pl.BlockSpec((1, HEAD_DIM), lambda i: (0, 0)), pl.BlockSpec((1, HEAD_DIM), lambda i: (0, 0))]
    return pl.pallas_call(
        functools.partial(_qk_prep_kernel, n_plain=n_plain_rows // bt), grid=(T // bt,), in_specs=in_specs,
        out_specs=[pl.BlockSpec((bt, q_w), row), pl.BlockSpec((bt, kv_w), row), pl.BlockSpec((bt, kv_w), row)],
        out_shape=[jax.ShapeDtypeStruct((T, q_w), BF16), jax.ShapeDtypeStruct((T, kv_w), BF16),
                   jax.ShapeDtypeStruct((T, kv_w), F32)],
        compiler_params=_cparams(("parallel",)), name="qk_norm_rope",
    )(u, u, u, u, cos, sin, q_norm, k_norm)


def _attn_kernel(q_ref, k_ref, v_ref, o_ref, *, heads, groups, scale):
    for h in range(heads):
        kv_sl = slice(h * HEAD_DIM, (h + 1) * HEAD_DIM)
        k = k_ref[:, kv_sl].astype(BF16)
        v = v_ref[:, kv_sl].astype(BF16)
        for g in range(groups):
            c = (h * groups + g) * HEAD_DIM
            q = q_ref[:, c:c + HEAD_DIM].astype(BF16)
            s = lax.dot_general(q, k, (((1,), (1,)), ((), ())), preferred_element_type=F32) * scale
            p = jnp.exp2(s - jnp.max(s, axis=-1, keepdims=True))
            l = jnp.sum(p, axis=-1, keepdims=True)
            o = jnp.dot(p.astype(BF16), v, preferred_element_type=F32) / l
            o_ref[:, c:c + HEAD_DIM] = o.astype(o_ref.dtype)


def _attention(q, k, v, *, batch, s_len, t_len, kv_heads, groups, bq, hp, q_row0=0, q_col0=0,
               k_row0=0, k_col0=0, v_col0=0):
    qw, kw = hp * groups * HEAD_DIM, hp * HEAD_DIM
    nq = s_len // bq
    assert q_row0 % bq == 0 and q_col0 % qw == 0 and k_row0 % t_len == 0 and kv_heads % hp == 0
    assert k_col0 % kw == 0 and v_col0 % kw == 0
    qr, qc, kr, kc, vc = q_row0 // bq, q_col0 // qw, k_row0 // t_len, k_col0 // kw, v_col0 // kw
    return pl.pallas_call(
        functools.partial(_attn_kernel, heads=hp, groups=groups, scale=HEAD_DIM ** -0.5 * LOG2E),
        grid=(batch, kv_heads // hp, nq),
        in_specs=[pl.BlockSpec((bq, qw), lambda b, h, i: (qr + b * nq + i, qc + h)),
                  pl.BlockSpec((t_len, kw), lambda b, h, i: (kr + b, kc + h)),
                  pl.BlockSpec((t_len, kw), lambda b, h, i: (kr + b, vc + h))],
        out_specs=pl.BlockSpec((bq, qw), lambda b, h, i: (b * nq + i, h)),
        out_shape=jax.ShapeDtypeStruct((batch * s_len, kv_heads * groups * HEAD_DIM), BF16),
        compiler_params=_cparams(("parallel", "parallel", "arbitrary")), name="attention",
    )(q, k, v)


def _pool_kernel(u_ref, w_ref, s_ref, o_ref, pad_ref, *, seq):
    g = pl.program_id(1)
    x = u_ref[...]
    pad_ref[0:POOL_HALO, :] = jnp.zeros((POOL_HALO, x.shape[1]), F32)
    pad_ref[POOL_HALO + seq:, :] = jnp.zeros((POOL_HALO, x.shape[1]), F32)
    pad_ref[POOL_HALO:POOL_HALO + seq, :] = x
    t = lax.broadcasted_iota(jnp.int32, x.shape, 0)
    for gi, win in enumerate(POOL_WINDOWS):
        @pl.when(g == gi)
        def _(win=win):
            back, fwd = win // 2, win - win // 2
            tot = pad_ref[POOL_HALO - back:POOL_HALO - back + seq, :]
            for j in range(-back + 1, fwd):
                tot = tot + pad_ref[POOL_HALO + j:POOL_HALO + j + seq, :]
            cnt = (jnp.minimum(t + fwd, seq) - jnp.maximum(t - back, 0)).astype(F32)
            diff = (tot / cnt - x).astype(BF16)
            y = jnp.dot(diff, w_ref[0], preferred_element_type=F32) * s_ref[...]
            o_ref[...] = y.astype(o_ref.dtype)


def _pool_mix(u, w_pool, pool_scale, *, n_seq, seq, row0):
    n_groups, cg, _ = w_pool.shape
    assert row0 % seq == 0 and max(POOL_WINDOWS) // 2 <= POOL_HALO and seq % 8 == 0
    r0 = row0 // seq
    return pl.pallas_call(
        functools.partial(_pool_kernel, seq=seq), grid=(n_seq, n_groups),
        in_specs=[pl.BlockSpec((seq, cg), lambda s, g: (r0 + s, g)),
                  pl.BlockSpec((1, cg, cg), lambda s, g: (g, 0, 0)),
                  pl.BlockSpec((1, cg), lambda s, g: (0, g))],
        out_specs=pl.BlockSpec((seq, cg), lambda s, g: (s, g)),
        out_shape=jax.ShapeDtypeStruct((n_seq * seq, n_groups * cg), BF16),
        scratch_shapes=[pltpu.VMEM((seq + 2 * POOL_HALO, cg), F32)],
        compiler_params=_cparams(("parallel", "arbitrary")), name="pool_mix",
    )(u, w_pool, pool_scale)


def _na_kernel(q_ref, k_ref, v_ref, kc_ref, vc_ref, t_ref, o_ref, bias_ref, *,
               rows_per_blk, key_rows, n_rows, wr, n_blk, scale):
    blk = pl.program_id(2)
    half = (key_rows - rows_per_blk) // 2
    W = GRID_W
    lane = lax.broadcasted_iota(jnp.int32, (W, 2 * W), 1)

    def build(b):
        k_start = min(max(b * rows_per_blk - half, 0), n_rows - key_rows)
        for qr in range(rows_per_blk):
            r = b * rows_per_blk + qr
            rs = min(max(r - wr // 2, 0), n_rows - wr)
            for kp in range(key_rows // 2):
                idx = [kk - r + NA_ROWS - 1 if rs <= kk < rs + wr else 2 * NA_ROWS - 1
                       for kk in (k_start + 2 * kp, k_start + 2 * kp + 1)]
                bias_ref[qr * W:(qr + 1) * W, kp * 2 * W:(kp + 1) * 2 * W] = jnp.where(
                    lane < W, t_ref[0, idx[0]], t_ref[0, idx[1]])

    pl.when(blk == 0)(functools.partial(build, 0))
    if n_blk > 2:
        pl.when(blk == 1)(functools.partial(build, 1))
    if n_blk > 1:
        pl.when(blk == n_blk - 1)(functools.partial(build, n_blk - 1))

    k_start = jnp.clip(blk * rows_per_blk - half, 0, n_rows - key_rows)
    tok0 = pl.multiple_of(k_start * W, W * 4)
    n_keys = key_rows * W
    q = q_ref[...].astype(BF16)
    kl = k_ref[pl.ds(tok0, n_keys), :].astype(BF16)
    vl = v_ref[pl.ds(tok0, n_keys), :].astype(BF16)
    kc = kc_ref[...].astype(BF16)
    vc = vc_ref[...].astype(BF16)
    dn = (((1,), (1,)), ((), ()))
    s_loc = lax.dot_general(q, kl, dn, preferred_element_type=F32) * scale + bias_ref[...]
    s_ctx = lax.dot_general(q, kc, dn, preferred_element_type=F32) * scale
    m = jnp.maximum(jnp.max(s_loc, axis=-1, keepdims=True), jnp.max(s_ctx, axis=-1, keepdims=True))
    p_loc = jnp.exp2(s_loc - m)
    p_ctx = jnp.exp2(s_ctx - m)
    l = jnp.sum(p_loc, axis=-1, keepdims=True) + jnp.sum(p_ctx, axis=-1, keepdims=True)
    o = (jnp.dot(p_ctx.astype(BF16), vc, preferred_element_type=F32)
         + jnp.dot(p_loc.astype(BF16), vl, preferred_element_type=F32)) / l
    o_ref[...] = o.astype(o_ref.dtype)


def _na_col_tables(rpb):
    H = rpb.shape[0]
    col = jnp.arange(GRID_W)
    cstart = jnp.clip(col - NA_COLS // 2, 0, GRID_W - NA_COLS)
    col_valid = (col[None, :] >= cstart[:, None]) & (col[None, :] < cstart[:, None] + NA_COLS)
    col_idx = jnp.clip(col[None, :] - col[:, None] + NA_COLS - 1, 0, 2 * NA_COLS - 2)
    onehot = (col_idx[None] == jnp.arange(2 * NA_COLS - 1)[:, None, None]).astype(F32)
    t = jnp.einsum("hrc,cqk->hrqk", rpb.astype(F32), onehot, precision=lax.Precision.HIGHEST)
    t = jnp.where(col_valid[None, None], t * LOG2E, NEG_INF)
    t = jnp.concatenate([t, jnp.full((H, 1, GRID_W, GRID_W), NEG_INF, F32)], axis=1)
    return jnp.concatenate([t, t], axis=-1)


def _na_attention(qkv, k_ctx, v_ctx, rpb, *, batch, n_tok, heads, row0, t_ctx):
    n_rows = n_tok // GRID_W
    wr = min(NA_ROWS, n_rows)
    rows_per_blk = min(NA_ROWS, n_rows)
    key_rows = min(rows_per_blk + wr, n_rows)
    n_blk = n_rows // rows_per_blk
    bq = rows_per_blk * GRID_W
    assert n_rows % rows_per_blk == 0 and row0 % n_tok == 0 and row0 % bq == 0 and key_rows % 2 == 0
    assert rows_per_blk % 4 == 0 and ((key_rows - rows_per_blk) // 2) % 4 == 0
    tables = _na_col_tables(rpb)
    qr0, kr0 = row0 // bq, row0 // n_tok
    return pl.pallas_call(
        functools.partial(_na_kernel, rows_per_blk=rows_per_blk, key_rows=key_rows, n_rows=n_rows, wr=wr,
                          n_blk=n_blk, scale=HEAD_DIM ** -0.5 * LOG2E),
        grid=(heads, batch, n_blk),
        in_specs=[pl.BlockSpec((bq, HEAD_DIM), lambda h, b, i: (qr0 + b * n_blk + i, h)),
                  pl.BlockSpec((n_tok, HEAD_DIM), lambda h, b, i: (kr0 + b, heads + h)),
                  pl.BlockSpec((n_tok, HEAD_DIM), lambda h, b, i: (kr0 + b, 2 * heads + h)),
                  pl.BlockSpec((t_ctx, HEAD_DIM), lambda h, b, i: (b, h)),
                  pl.BlockSpec((t_ctx, HEAD_DIM), lambda h, b, i: (b, h)),
                  pl.BlockSpec((1, 2 * NA_ROWS, GRID_W, 2 * GRID_W), lambda h, b, i: (h, 0, 0, 0))],
        out_specs=pl.BlockSpec((bq, HEAD_DIM), lambda h, b, i: (b * n_blk + i, h)),
        out_shape=jax.ShapeDtypeStruct((batch * n_tok, heads * HEAD_DIM), BF16),
        scratch_shapes=[pltpu.VMEM((bq, key_rows * GRID_W), F32)],
        compiler_params=_cparams(("arbitrary", "arbitrary", "arbitrary")), name="na_attention",
    )(qkv, qkv, qkv, k_ctx, v_ctx, tables)


def _row_copy(src_hbm, row, dst, slot, sem):
    return pltpu.make_async_copy(src_hbm.at[pl.ds(row, 1), :], dst.at[pl.ds(slot, 1), :], sem)


def _gather_kernel(src_ref, x_hbm, o_ref, buf, sem, *, rt, n_steps):
    i = pl.program_id(0)
    cur = i % 2

    def issue(step, s):
        def body(r, c):
            _row_copy(x_hbm, src_ref[step * rt + r], buf.at[s], r, sem.at[s]).start()
            return c
        lax.fori_loop(0, rt, body, 0, unroll=8)

    @pl.when(i == 0)
    def _():
        issue(0, 0)

    @pl.when(i + 1 < n_steps)
    def _():
        issue(i + 1, 1 - cur)

    pltpu.make_async_copy(x_hbm.at[pl.ds(0, rt), :], buf.at[cur], sem.at[cur]).wait()
    o_ref[...] = buf[cur].astype(o_ref.dtype)


def _gather_rows(x, src, *, rt):
    n = src.shape[0]
    D = x.shape[1]
    return pl.pallas_call(
        functools.partial(_gather_kernel, rt=rt, n_steps=n // rt),
        out_shape=jax.ShapeDtypeStruct((n, D), BF16),
        grid_spec=pltpu.PrefetchScalarGridSpec(
            num_scalar_prefetch=1, grid=(n // rt,),
            in_specs=[pl.BlockSpec(memory_space=pl.ANY)],
            out_specs=pl.BlockSpec((rt, D), lambda i, s: (i, 0)),
            scratch_shapes=[pltpu.VMEM((2, rt, D), F32), pltpu.SemaphoreType.DMA((2,))]),
        compiler_params=_cparams(("arbitrary",)), name="moe_dispatch",
    )(src, x)


def _combine_kernel(pos_ref, x_ref, g_ref, w_ref, fn_ref, y_hbm, oa_ref, ob_ref, buf, sem, *, ct, n_steps, n_first):
    i = pl.program_id(0)
    cur = i % 2

    def issue(step, s):
        def body(r, c):
            for kk in range(TOP_K):
                _row_copy(y_hbm, pos_ref[TOP_K * (step * ct + r) + kk], buf.at[s, kk], r, sem.at[s]).start()
            return c
        lax.fori_loop(0, ct, body, 0, unroll=8)

    @pl.when(i == 0)
    def _():
        issue(0, 0)

    @pl.when(i + 1 < n_steps)
    def _():
        issue(i + 1, 1 - cur)

    for kk in range(TOP_K):
        pltpu.make_async_copy(y_hbm.at[pl.ds(0, ct), :], buf.at[cur, kk], sem.at[cur]).wait()
    w = w_ref[...]
    acc = w[:, 0:1] * buf[cur, 0]
    for kk in range(1, TOP_K):
        acc = acc + w[:, kk:kk + 1] * buf[cur, kk]
    out = _rms(x_ref[...] + g_ref[0] * acc, fn_ref[...])

    @pl.when(i < n_first)
    def _():
        oa_ref[...] = out

    @pl.when(i >= n_first)
    def _():
        ob_ref[...] = out


def _combine_norm(x, gate, top_w, final_g, y, pos, *, ct, gate_rows, rows_a):
    T, D = x.shape
    n_first = rows_a // ct
    assert n_first * ct == rows_a and 0 < n_first < T // ct
    return pl.pallas_call(
        functools.partial(_combine_kernel, ct=ct, n_steps=T // ct, n_first=n_first),
        out_shape=[jax.ShapeDtypeStruct((rows_a, D), F32), jax.ShapeDtypeStruct((T - rows_a, D), F32)],
        grid_spec=pltpu.PrefetchScalarGridSpec(
            num_scalar_prefetch=1, grid=(T // ct,),
            in_specs=[pl.BlockSpec((ct, D), lambda i, p: (i, 0)),
                      pl.BlockSpec((1, 1, D), lambda i, p: ((i * ct) // gate_rows, 0, 0)),
                      pl.BlockSpec((ct, LANES), lambda i, p: (i, 0)),
                      pl.BlockSpec((1, D), lambda i, p: (0, 0)),
                      pl.BlockSpec(memory_space=pl.ANY)],
            out_specs=[pl.BlockSpec((ct, D), lambda i, p: (jnp.minimum(i, n_first - 1), 0)),
                       pl.BlockSpec((ct, D), lambda i, p: (jnp.maximum(i - n_first, 0), 0))],
            scratch_shapes=[pltpu.VMEM((2, TOP_K, ct, D), F32), pltpu.SemaphoreType.DMA((2,))]),
        compiler_params=_cparams(("arbitrary",)), name="moe_combine",
    )(pos, x, gate, top_w, final_g, y)


def _routing_plan(top_i, n_experts, tm):
    T = top_i.shape[0]
    e_flat = top_i.reshape(-1)
    onehot = (e_flat[:, None] == jnp.arange(n_experts)[None, :]).astype(jnp.int32)
    counts = jnp.sum(onehot, axis=0)
    rank = jnp.sum((jnp.cumsum(onehot, axis=0) - onehot) * onehot, axis=1)
    gsz = ((counts + tm - 1) // tm) * tm
    gend = jnp.cumsum(gsz)
    goff = gend - gsz
    pos = (goff[e_flat] + rank).astype(jnp.int32)
    n_tiles = (T * TOP_K) // tm + n_experts
    src = jnp.zeros((n_tiles * tm,), jnp.int32).at[pos].set(jnp.arange(T * TOP_K, dtype=jnp.int32) // TOP_K)
    n_valid = (gend[-1] // tm).astype(jnp.int32)
    tile_row = jnp.minimum(jnp.arange(n_tiles, dtype=jnp.int32), n_valid - 1) * tm
    tile_expert = jnp.sum((tile_row[:, None] >= gend[None, :]).astype(jnp.int32), axis=1)
    tile_rows = jnp.clip((goff + counts)[tile_expert] - tile_row, 0, tm)
    return pos, src, tile_expert.astype(jnp.int32), n_valid.reshape(1), tile_rows.astype(jnp.int32)


def _pick(n, prefs):
    for p in prefs:
        if n % p == 0:
            return p
    return n


def _modulation(cvecs, w, b):
    n_c, D = cvecs.shape
    rows = 8
    a = jnp.zeros((rows, D), BF16).at[:n_c].set(jax.nn.silu(cvecs).astype(BF16))
    m = _matmul(a, (w[None],), bm=rows, bn=_pick(6 * D, (1024, 512, 256, 128)), bk=D,
                mode="bias", extra=(b[None, :],), name="modulation")
    return jnp.transpose(m[:n_c].reshape(n_c, 6, 1, D), (1, 0, 2, 3))


def _rope_tables(n_prompt_rows, dec_batch, dec_seq):
    t = jnp.arange(dec_seq)
    pos = jnp.stack([t // GRID_W, t % GRID_W], axis=-1).astype(F32)
    n_freq = HEAD_DIM // 4
    inv = ROPE_THETA ** (-jnp.arange(n_freq, dtype=F32) / n_freq)
    ang = pos[:, :, None] * inv
    cos = jnp.concatenate([jnp.cos(ang)] * 2, axis=-1).reshape(dec_seq, HEAD_DIM)
    sin = jnp.concatenate([-jnp.sin(ang), jnp.sin(ang)], axis=-1).reshape(dec_seq, HEAD_DIM)
    cos = jnp.concatenate([jnp.ones((n_prompt_rows, HEAD_DIM), F32)] + [cos] * dec_batch)
    sin = jnp.concatenate([jnp.zeros((n_prompt_rows, HEAD_DIM), F32)] + [sin] * dec_batch)
    return cos, sin


def kernel(x_prompt, x_sample, c, c_ctx, cache_l0_attn_k, cache_l0_attn_v, cache_l1_na_k, cache_l1_na_v,
           l0_ada_w, l0_ada_b, l0_norm1, l0_w_in, l0_q_norm, l0_k_norm, l0_w_pool, l0_pool_scale, l0_w_out,
           l0_norm2, l0_ffn_w1, l0_ffn_w3, l0_ffn_w2,
           l1_ada_w, l1_ada_b, l1_norm1, l1_w_qkv, l1_rpb, l1_w_out, l1_norm2, l1_router,
           l1_exp_w1, l1_exp_w3, l1_exp_w2, final_norm):
    batch, seq, D = x_prompt.shape
    dec_batch, dec_seq, _ = x_sample.shape
    past_len = cache_l0_attn_k.shape[1]
    kv_heads0 = cache_l0_attn_k.shape[2]
    heads1 = cache_l1_na_k.shape[2]
    n_pool, cg, _ = l0_w_pool.shape
    pool_w = n_pool * cg
    kv_w = kv_heads0 * HEAD_DIM
    q_w = l0_w_in.shape[1] - pool_w - 2 * kv_w
    groups0 = q_w // kv_w
    c_w = heads1 * HEAD_DIM
    n_experts = l1_router.shape[1]
    Tp, Ts = batch * seq, dec_batch * dec_seq
    T = Tp + Ts
    group_rows = math.gcd(Tp, dec_seq)
    grp_to_c = jnp.array([0] * (Tp // group_rows) + [1 + b for b in range(dec_batch)
                                                      for _ in range(dec_seq // group_rows)], jnp.int32)

    bt = _pick(group_rows, (256, 128, 64, 32, 16, 8))
    bm = _pick(group_rows, (1024, 512, 256, 128, 64, 32, 16, 8))
    bm2 = _pick(group_rows, (512, 256, 128, 64, 32, 16, 8))
    blk = lambda n: _pick(n, (512, 256, 128))
    seq_blk = lambda n: _pick(n, (256, 128, 64, 32, 16, 8))
    mods = lambda m: [m[i][grp_to_c] for i in range(6)]
    cvecs = jnp.concatenate([c_ctx[None, :], c], axis=0)
    gg = dict(gate_rows=group_rows)

    xa, xb = x_prompt.reshape(Tp, D), x_sample.reshape(Ts, D)

    sh1, sc1, g1, sh2, sc2, g2 = mods(_modulation(cvecs, l0_ada_w, l0_ada_b))
    h = _adaln_stacked(xa, xb, l0_norm1[None], sc1, sh1, group_rows=group_rows, bt=bt, out_dtype=BF16)
    bn_in = blk(l0_w_in.shape[1])
    u, ffn_w2 = _matmul_with_cast(h, (l0_w_in[None],), l0_ffn_w2, bm=bm, bn=bn_in, bk=D, name="l0_in_proj")
    cos, sin = _rope_tables(Tp, dec_batch, dec_seq)
    q_att, k_att, k_state = _qk_prep(u, cos, sin, l0_q_norm[None], l0_k_norm[None],
                                     pool_w=pool_w, q_w=q_w, kv_w=kv_w, bt=bt, n_plain_rows=Tp)
    v_col0 = pool_w + q_w + kv_w
    att_p = _attention(q_att, k_att, u, batch=batch, s_len=seq, t_len=seq, kv_heads=kv_heads0, groups=groups0,
                       bq=seq_blk(seq), hp=kv_heads0, v_col0=v_col0)
    k_lat = jnp.concatenate([cache_l0_attn_k.reshape(dec_batch, past_len, kv_w).astype(BF16),
                             k_att[Tp:].reshape(dec_batch, dec_seq, kv_w)], axis=1).reshape(-1, kv_w)
    v_lat = jnp.concatenate([cache_l0_attn_v.reshape(dec_batch, past_len, kv_w),
                             u[Tp:, v_col0:].reshape(dec_batch, dec_seq, kv_w)], axis=1).reshape(-1, kv_w)
    att_s = _attention(q_att, k_lat, v_lat, batch=dec_batch, s_len=dec_seq, t_len=past_len + dec_seq,
                       kv_heads=kv_heads0, groups=groups0, bq=seq_blk(dec_seq), hp=1, q_row0=Tp)
    w_pool = l0_w_pool.astype(BF16)
    pool_p = _pool_mix(u, w_pool, l0_pool_scale[None], n_seq=batch, seq=seq, row0=0)
    pool_s = _pool_mix(u, w_pool, l0_pool_scale[None], n_seq=dec_batch, seq=dec_seq, row0=Tp)
    x = _matmul([(pool_p, pool_s), (att_p, att_s)], (l0_w_out[None],), bm=bm2, bn=blk(D), bk=pool_w + q_w,
                mode="gated_res", extra=(xa, xb, g1), n_first=Tp // bm2, name="l0_out_proj", **gg)
    h = _adaln(x, l0_norm2[None], sc2, sh2, group_rows=group_rows, bt=bt, out_dtype=BF16)
    d_ff = l0_ffn_w1.shape[1]
    d_fe = l1_exp_w1.shape[2]
    hh, exp_w2 = _matmul_with_cast(h, (l0_ffn_w1[None], l0_ffn_w3[None]), l1_exp_w2.reshape(n_experts * d_fe, D),
                                   bm=bm, bn=blk(d_ff), bk=D, mode="swiglu", out_dtype=BF16, name="l0_ffn_up")
    x = _matmul(hh, (ffn_w2[None],), bm=bm, bn=blk(D),
                bk=_pick(d_ff, (5504, 2048, 1024, 512, 256, 128)),
                mode="gated_res", extra=(x, g2), name="l0_ffn_down", **gg)
    state_l0_k = k_state[:Tp].reshape(batch, seq, kv_heads0, HEAD_DIM)
    state_l0_v = u[:Tp, v_col0:].reshape(batch, seq, kv_heads0, HEAD_DIM)

    sh1, sc1, g1, sh2, sc2, g2 = mods(_modulation(cvecs, l1_ada_w, l1_ada_b))
    h = _adaln(x, l1_norm1[None], sc1, sh1, group_rows=group_rows, bt=bt, out_dtype=BF16)
    qkv = _matmul(h, (l1_w_qkv[None],), bm=bm, bn=blk(3 * c_w), bk=D, name="l1_qkv_proj")
    att_p = _attention(qkv, qkv, qkv, batch=batch, s_len=seq, t_len=seq, kv_heads=heads1, groups=1,
                       bq=seq_blk(seq), hp=_pick(heads1, (8, 4, 2, 1)), k_col0=c_w, v_col0=2 * c_w)
    att_s = _na_attention(qkv, cache_l1_na_k.reshape(dec_batch * past_len, c_w),
                          cache_l1_na_v.reshape(dec_batch * past_len, c_w), l1_rpb,
                          batch=dec_batch, n_tok=dec_seq, heads=heads1, row0=Tp, t_ctx=past_len)
    x = _matmul([(att_p, att_s)], (l1_w_out[None],), bm=bm2, bn=blk(D), bk=c_w,
                mode="gated_res", extra=(x, g1), n_first=Tp // bm2, name="l1_out_proj", **gg)
    h32, top_i, top_w = _adaln(x, l1_norm2[None], sc2, sh2, group_rows=group_rows, bt=bt, out_dtype=F32,
                               router=l1_router)
    tm = _pick(T * TOP_K, (512, 256, 128, 64, 32, 16, 8))
    sub = LANES if tm % LANES == 0 else tm
    pos, src, tile_expert, n_valid, tile_rows = _routing_plan(top_i[:, :TOP_K], n_experts, tm)
    routed = dict(tile_group=tile_expert, n_valid=n_valid, tile_rows=tile_rows, sub=sub)
    xs = _gather_rows(h32, src, rt=_pick(tm, (256, 128, 64, 32, 16, 8)))
    he = _moe_up(xs, l1_exp_w1, l1_exp_w3, bm=tm, bn=blk(d_fe), **routed)
    ye = _matmul(he, (exp_w2.reshape(n_experts, d_fe, D),), bm=tm, bn=_pick(D, (1024, 512, 256, 128)),
                 bk=_pick(d_fe, (7168, 3584, 2048, 1024, 512, 256, 128)), name="moe_down", **routed)
    y_p, y_s = _combine_norm(x, g2, top_w, final_norm[None], ye, pos, ct=_pick(group_rows, (128, 64, 32, 16, 8)),
                             gate_rows=group_rows, rows_a=Tp)
    state_l1_k = qkv[:Tp, c_w:2 * c_w].reshape(batch, seq, heads1, HEAD_DIM)
    state_l1_v = qkv[:Tp, 2 * c_w:].reshape(batch, seq, heads1, HEAD_DIM)

    return (y_p.reshape(batch, seq, D), y_s.reshape(dec_batch, dec_seq, D),
            state_l0_k, state_l0_v, state_l1_k, state_l1_v)
```

```python
import functools
import math

import jax
import jax.numpy as jnp
from jax import lax
from jax.experimental import pallas as pl
from jax.experimental.pallas import tpu as pltpu

HEAD_DIM = 128
GRID_W = 64
POOL_WINDOWS = (2, 4, 8, 16)
NA_ROWS = 8
NA_COLS = 16
TOP_K = 2
EPS = 1e-6
ROPE_THETA = 10000.0
NEG_INF = -1e30
LOG2E = math.log2(math.e)
LANES = 128
POOL_HALO = 16
VMEM_LIMIT_BYTES = 60 * 1024 * 1024

F32 = jnp.float32
BF16 = jnp.bfloat16


def _cparams(sem):
    return pltpu.CompilerParams(dimension_semantics=sem, vmem_limit_bytes=VMEM_LIMIT_BYTES)


def _partial_tile(rows, bm, sub, k_last, compute, o_ref):
    @pl.when(rows > sub)
    def _():
        compute(slice(0, bm))

    @pl.when(rows <= sub)
    def _():
        compute(slice(0, sub))
        if sub < bm:
            @pl.when(k_last)
            def _():
                o_ref[sub:, :] = jnp.zeros((bm - sub, o_ref.shape[1]), o_ref.dtype)


def _mm_kernel(grp_ref, nvalid_ref, rows_ref, *refs, nk, mode, cast_w, sub, n_first, seg_stacked, res_stacked,
               side):
    n_w = 2 if mode == "swiglu" else 1
    n_a = sum(2 if st else 1 for st in seg_stacked)
    a_refs = refs[:n_a]
    w_refs = refs[n_a:n_a + n_w]
    n_extra = {"plain": 0, "swiglu": 0, "bias": 1, "gated_res": 3 if res_stacked else 2}[mode]
    pos = n_a + n_w
    extra = refs[pos:pos + n_extra]
    pos += n_extra
    if side:
        side_in, o_ref, side_out = refs[pos], refs[pos + 1], refs[pos + 2]
        pos += 3
    else:
        o_ref = refs[pos]
        pos += 1
    scratch = refs[pos:]
    acc_in_out = nk > 1 and not scratch
    bm = a_refs[0].shape[0]
    m = pl.program_id(1)
    k = pl.program_id(2)

    def a_segments(rs):
        out, i = [], 0
        for st in seg_stacked:
            if st:
                out.append(jnp.where(m < n_first, a_refs[i][rs, :], a_refs[i + 1][rs, :]))
                i += 2
            else:
                out.append(a_refs[i][rs, :])
                i += 1
        return out

    def epilogue(accs, rs):
        if mode == "plain":
            r = accs[0]
        elif mode == "swiglu":
            g = accs[0]
            r = (g / (1.0 + jnp.exp(-g))) * accs[1]
        elif mode == "bias":
            r = accs[0] + extra[0][...]
        else:
            res = jnp.where(m < n_first, extra[0][rs, :], extra[1][rs, :]) if res_stacked else extra[0][rs, :]
            r = res + extra[-1][0] * accs[0]
        o_ref[rs, :] = r.astype(o_ref.dtype)

    def compute(rs):
        w_of = (lambda j: scratch[j]) if cast_w else (lambda j: w_refs[j].at[0])
        segs = a_segments(rs)
        parts = []
        for j in range(n_w):
            acc, off = None, 0
            for a in segs:
                d = jnp.dot(a, w_of(j)[off:off + a.shape[1], :], preferred_element_type=F32)
                acc = d if acc is None else acc + d
                off += a.shape[1]
            parts.append(acc)
        if side:
            side_out[...] = side_in[...].astype(BF16)
        if nk == 1:
            epilogue(parts, rs)
        elif acc_in_out:
            @pl.when(k == 0)
            def _():
                o_ref[rs, :] = parts[0]

            @pl.when(k > 0)
            def _():
                o_ref[rs, :] += parts[0]
        else:
            @pl.when(k == 0)
            def _():
                for acc, p in zip(scratch, parts):
                    acc[rs, :] = p

            @pl.when(k > 0)
            def _():
                for acc, p in zip(scratch, parts):
                    acc[rs, :] += p

            @pl.when(k == nk - 1)
            def _():
                epilogue([acc[rs, :] for acc in scratch], rs)

    @pl.when(m < nvalid_ref[0])
    def _():
        if cast_w:
            @pl.when(jnp.logical_or(m == 0, grp_ref[m] != grp_ref[jnp.maximum(m - 1, 0)]))
            def _():
                for w_ref, wb in zip(w_refs, scratch):
                    wb[...] = w_ref[0].astype(BF16)
        if sub is None:
            compute(slice(0, bm))
        else:
            _partial_tile(rows_ref[m], bm, sub, k == nk - 1, compute, o_ref)

    @pl.when(jnp.logical_and(m >= nvalid_ref[0], k == nk - 1))
    def _():
        o_ref[...] = jnp.zeros(o_ref.shape, o_ref.dtype)


def _matmul(a, ws, *, bm, bn, bk, mode="plain", out_dtype=F32, extra=(), tile_group=None,
            n_valid=None, tile_rows=None, sub=None, gate_rows=None, n_first=None, side=None, name="mm"):
    segs = a if isinstance(a, (list, tuple)) else [a]
    seg_stacked = tuple(isinstance(sg, tuple) for sg in segs)
    seg_rows = lambda sg: sum(p.shape[0] for p in sg) if isinstance(sg, tuple) else sg.shape[0]
    seg_cols = lambda sg: sg[0].shape[1] if isinstance(sg, tuple) else sg.shape[1]
    M, K = seg_rows(segs[0]), sum(seg_cols(sg) for sg in segs)
    E, _, N = ws[0].shape
    nm, nn, nk = M // bm, N // bn, K // bk
    assert nm * bm == M and nn * bn == N and nk * bk == K, (M, K, ws[0].shape, bm, bn, bk)
    assert all(seg_rows(sg) == M for sg in segs) and (nk == 1 or len(segs) == 1 and not seg_stacked[0])
    cast_w = ws[0].dtype == F32
    assert not (cast_w and nk > 1) and not (side is not None and (sub is not None or nk > 1))
    if tile_group is None:
        tile_group = jnp.zeros((nm,), jnp.int32)
        n_valid = jnp.full((1,), nm, jnp.int32)
    if tile_rows is None:
        tile_rows = jnp.full((nm,), bm, jnp.int32)
    kk = lambda m, k: jnp.where(m % 2 == 1, nk - 1 - k, k) if nk > 1 else k
    first_rows = lambda n, m, k, *_: (jnp.minimum(m, n_first - 1), 0)
    second_rows = lambda n, m, k, *_: (jnp.maximum(m - n_first, 0), 0)
    in_specs, a_args = [], []
    for sg in segs:
        if isinstance(sg, tuple):
            assert sg[0].shape[0] == n_first * bm
            in_specs += [pl.BlockSpec((bm, sg[0].shape[1]), first_rows), pl.BlockSpec((bm, sg[1].shape[1]), second_rows)]
            a_args += list(sg)
        else:
            in_specs.append(pl.BlockSpec((bm, bk if nk > 1 else sg.shape[1]), lambda n, m, k, *_: (m, kk(m, k))))
            a_args.append(sg)
    for _ in ws:
        in_specs.append(pl.BlockSpec((1, bk, bn), lambda n, m, k, g, *_: (g[m], kk(m, k), n)))
    if mode == "bias":
        in_specs.append(pl.BlockSpec((1, bn), lambda n, m, k, *_: (0, n)))
    elif mode == "gated_res":
        if len(extra) == 2:
            in_specs.append(pl.BlockSpec((bm, bn), lambda n, m, k, *_: (m, n)))
        else:
            in_specs.append(pl.BlockSpec((bm, bn), lambda n, m, k, *_: (jnp.minimum(m, n_first - 1), n)))
            in_specs.append(pl.BlockSpec((bm, bn), lambda n, m, k, *_: (jnp.maximum(m - n_first, 0), n)))
        in_specs.append(pl.BlockSpec((1, 1, bn), lambda n, m, k, *_: ((m * bm) // gate_rows, 0, n)))
    if cast_w:
        scratch = [pltpu.VMEM((bk, bn), BF16) for _ in ws]
    elif nk == 1 or (mode == "plain" and out_dtype == F32):
        scratch = []
    else:
        scratch = [pltpu.VMEM((bm, bn), F32) for _ in ws]
    out_shape = jax.ShapeDtypeStruct((M, N), out_dtype)
    out_specs = pl.BlockSpec((bm, bn), lambda n, m, k, *_: (m, n))
    side_args = ()
    if side is not None:
        src, rows = side
        n_side = src.shape[0] // rows
        assert n_side * rows == src.shape[0] and n_side <= nn * nm
        side_spec = pl.BlockSpec((rows, src.shape[1]), lambda n, m, k, *_: (jnp.minimum(n * nm + m, n_side - 1), 0))
        in_specs.append(side_spec)
        out_shape = [out_shape, jax.ShapeDtypeStruct(src.shape, BF16)]
        out_specs = [out_specs, side_spec]
        side_args = (src,)
    return pl.pallas_call(
        functools.partial(_mm_kernel, nk=nk, mode=mode, cast_w=cast_w, sub=sub, n_first=n_first,
                          seg_stacked=seg_stacked, res_stacked=mode == "gated_res" and len(extra) == 3,
                          side=side is not None),
        out_shape=out_shape,
        grid_spec=pltpu.PrefetchScalarGridSpec(
            num_scalar_prefetch=3,
            grid=(nn, nm, nk),
            in_specs=in_specs,
            out_specs=out_specs,
            scratch_shapes=scratch),
        compiler_params=_cparams(("arbitrary", "arbitrary", "arbitrary")),
        name=name,
    )(tile_group, n_valid, tile_rows, *a_args, *ws, *extra, *side_args)


def _matmul_with_cast(a, ws, src, *, bm, bn, **kw):
    steps = (a.shape[0] // bm) * (ws[0].shape[2] // bn)
    R = src.shape[0]
    rows = next((r for r in range(16, R + 1, 16) if R % r == 0 and R // r <= steps), None)
    if rows is None:
        return _matmul(a, ws, bm=bm, bn=bn, **kw), src.astype(BF16)
    out, cast = _matmul(a, ws, bm=bm, bn=bn, side=(src, rows), **kw)
    return out, cast


def _moe_up_kernel(grp_ref, nvalid_ref, rows_ref, a_ref, w1a_ref, w3a_ref, w1b_ref, w3b_ref, o_ref,
                   wa_ref, wb_ref, par_ref, *, nm, kh, sub):
    n = pl.program_id(0)
    s = pl.program_id(1)
    m = s - 1
    first = jnp.logical_and(n == 0, s == 0)
    g_cur = grp_ref[jnp.clip(m, 0, nm - 1)]
    g_prev = grp_ref[jnp.clip(m - 1, 0, nm - 1)]
    g_next = grp_ref[jnp.clip(m + 1, 0, nm - 1)]

    @pl.when(first)
    def _():
        par_ref[0] = 0

    @pl.when(jnp.logical_or(s == 1, jnp.logical_and(s > 1, g_cur != g_prev)))
    def _():
        par_ref[0] = 1 - par_ref[0]
        wb_ref[0] = w1b_ref[0].astype(BF16)
        wb_ref[1] = w3b_ref[0].astype(BF16)

    p = par_ref[0]

    nxt_new = jnp.logical_or(jnp.logical_or(first, s == nm),
                             jnp.logical_and(jnp.logical_and(s >= 1, s < nm), g_next != g_cur))

    @pl.when(nxt_new)
    def _():
        wa_ref[1 - p, 0] = w1a_ref[0].astype(BF16)
        wa_ref[1 - p, 1] = w3a_ref[0].astype(BF16)

    def compute(rs):
        a_lo = a_ref[rs, :kh]
        a_hi = a_ref[rs, kh:]
        g = (jnp.dot(a_lo, wa_ref[p, 0], preferred_element_type=F32)
             + jnp.dot(a_hi, wb_ref[0], preferred_element_type=F32))
        u = (jnp.dot(a_lo, wa_ref[p, 1], preferred_element_type=F32)
             + jnp.dot(a_hi, wb_ref[1], preferred_element_type=F32))
        o_ref[rs, :] = ((g / (1.0 + jnp.exp(-g))) * u).astype(o_ref.dtype)

    @pl.when(jnp.logical_and(s >= 1, m < nvalid_ref[0]))
    def _():
        _partial_tile(rows_ref[jnp.clip(m, 0, nm - 1)], a_ref.shape[0], sub, True, compute, o_ref)

    @pl.when(jnp.logical_and(s >= 1, m >= nvalid_ref[0]))
    def _():
        o_ref[...] = jnp.zeros(o_ref.shape, o_ref.dtype)


def _moe_up(a, w1, w3, *, bm, bn, sub, tile_group, n_valid, tile_rows):
    M, K = a.shape
    E, _, N = w1.shape
    nm, nn, kh = M // bm, N // bn, K // 2
    assert nm * bm == M and nn * bn == N and kh * 2 == K and kh % LANES == 0
    tile = lambda s: jnp.maximum(s - 1, 0)

    def w_ahead(n, s, g, *_):
        wrap = s == nm
        return g[jnp.where(wrap, 0, jnp.minimum(s, nm - 1))], 0, jnp.where(wrap, jnp.minimum(n + 1, nn - 1), n)

    w_now = lambda n, s, g, *_: (g[tile(s)], 1, n)
    return pl.pallas_call(
        functools.partial(_moe_up_kernel, nm=nm, kh=kh, sub=sub),
        out_shape=jax.ShapeDtypeStruct((M, N), BF16),
        grid_spec=pltpu.PrefetchScalarGridSpec(
            num_scalar_prefetch=3,
            grid=(nn, nm + 1),
            in_specs=[pl.BlockSpec((bm, K), lambda n, s, *_: (tile(s), 0)),
                      pl.BlockSpec((1, kh, bn), w_ahead), pl.BlockSpec((1, kh, bn), w_ahead),
                      pl.BlockSpec((1, kh, bn), w_now), pl.BlockSpec((1, kh, bn), w_now)],
            out_specs=pl.BlockSpec((bm, bn), lambda n, s, *_: (tile(s), n)),
            scratch_shapes=[pltpu.VMEM((2, 2, kh, bn), BF16), pltpu.VMEM((2, kh, bn), BF16),
                            pltpu.SMEM((1,), jnp.int32)]),
        compiler_params=_cparams(("arbitrary", "arbitrary")),
        name="moe_up",
    )(tile_group, n_valid, tile_rows, a, w1, w3, w1, w3)


def _rms(x, g):
    return x * lax.rsqrt(jnp.mean(x * x, axis=-1, keepdims=True) + EPS) * g


def _adaln_kernel(x_ref, g_ref, sc_ref, sh_ref, o_ref):
    y = _rms(x_ref[...], g_ref[...])
    o_ref[...] = (y * (1.0 + sc_ref[0]) + sh_ref[0]).astype(o_ref.dtype)


def _adaln_stacked_kernel(xa_ref, xb_ref, g_ref, sc_ref, sh_ref, o_ref, *, n_first):
    x = jnp.where(pl.program_id(0) < n_first, xa_ref[...], xb_ref[...])
    y = _rms(x, g_ref[...])
    o_ref[...] = (y * (1.0 + sc_ref[0]) + sh_ref[0]).astype(o_ref.dtype)


def _adaln_stacked(xa, xb, g, scale, shift, *, group_rows, bt, out_dtype):
    Ta, D = xa.shape
    T = Ta + xb.shape[0]
    n_first = Ta // bt
    assert n_first * bt == Ta and xb.shape[0] % bt == 0
    mod_spec = pl.BlockSpec((1, 1, D), lambda i: ((i * bt) // group_rows, 0, 0))
    return pl.pallas_call(
        functools.partial(_adaln_stacked_kernel, n_first=n_first), grid=(T // bt,),
        in_specs=[pl.BlockSpec((bt, D), lambda i: (jnp.minimum(i, n_first - 1), 0)),
                  pl.BlockSpec((bt, D), lambda i: (jnp.maximum(i - n_first, 0), 0)),
                  pl.BlockSpec((1, D), lambda i: (0, 0)), mod_spec, mod_spec],
        out_specs=pl.BlockSpec((bt, D), lambda i: (i, 0)),
        out_shape=jax.ShapeDtypeStruct((T, D), out_dtype),
        compiler_params=_cparams(("parallel",)), name="adaln",
    )(xa, xb, g, scale, shift)


def _adaln_router_kernel(x_ref, g_ref, sc_ref, sh_ref, r_ref, o_ref, idx_ref, w_ref, *, n_experts):
    y = _rms(x_ref[...], g_ref[...])
    h = y * (1.0 + sc_ref[0]) + sh_ref[0]
    o_ref[...] = h
    logits = jnp.dot(h, r_ref[...], preferred_element_type=F32, precision=lax.Precision.HIGHEST)
    lane = lax.broadcasted_iota(jnp.int32, logits.shape, 1).astype(F32)
    lg = jnp.where(lane < n_experts, logits, -jnp.inf)
    v1 = jnp.max(lg, axis=-1, keepdims=True)
    i1 = jnp.min(jnp.where(lg == v1, lane, float(LANES)), axis=-1, keepdims=True)
    lg2 = jnp.where(lane == i1, -jnp.inf, lg)
    v2 = jnp.max(lg2, axis=-1, keepdims=True)
    i2 = jnp.min(jnp.where(lg2 == v2, lane, float(LANES)), axis=-1, keepdims=True)
    e = jnp.exp(v2 - v1)
    den = 1.0 + e
    idx_ref[...] = jnp.where(lane == 0, i1, jnp.where(lane == 1, i2, 0.0)).astype(jnp.int32)
    w_ref[...] = jnp.where(lane == 0, 1.0 / den, jnp.where(lane == 1, e / den, 0.0))


def _adaln(x, g, scale, shift, *, group_rows, bt, out_dtype, router=None):
    T, D = x.shape
    grid = (T // bt,)
    x_spec = pl.BlockSpec((bt, D), lambda i: (i, 0))
    g_spec = pl.BlockSpec((1, D), lambda i: (0, 0))
    mod_spec = pl.BlockSpec((1, 1, D), lambda i: ((i * bt) // group_rows, 0, 0))
    if router is None:
        return pl.pallas_call(
            _adaln_kernel, grid=grid, in_specs=[x_spec, g_spec, mod_spec, mod_spec], out_specs=x_spec,
            out_shape=jax.ShapeDtypeStruct((T, D), out_dtype),
            compiler_params=_cparams(("parallel",)), name="adaln",
        )(x, g, scale, shift)
    n_experts = router.shape[1]
    r_pad = jnp.pad(router, ((0, 0), (0, LANES - n_experts)))
    lane_spec = pl.BlockSpec((bt, LANES), lambda i: (i, 0))
    return pl.pallas_call(
        functools.partial(_adaln_router_kernel, n_experts=n_experts), grid=grid,
        in_specs=[x_spec, g_spec, mod_spec, mod_spec, pl.BlockSpec((D, LANES), lambda i: (0, 0))],
        out_specs=[x_spec, lane_spec, lane_spec],
        out_shape=[jax.ShapeDtypeStruct((T, D), F32), jax.ShapeDtypeStruct((T, LANES), jnp.int32),
                   jax.ShapeDtypeStruct((T, LANES), F32)],
        compiler_params=_cparams(("parallel",)), name="adaln_router",
    )(x, g, scale, shift, r_pad)


def _qk_prep_kernel(q0_ref, q1_ref, q2_ref, k_ref, cos_ref, sin_ref, qn_ref, kn_ref,
                    qo_ref, ko_ref, ks_ref, *, n_plain):
    cos = cos_ref[...]
    sin = sin_ref[...]
    lane = lax.broadcasted_iota(jnp.int32, cos.shape, 1)
    first_half = (lane % (HEAD_DIM // 2)) < (HEAD_DIM // 4)

    def rope(y):
        partner = jnp.where(first_half, pltpu.roll(y, HEAD_DIM - HEAD_DIM // 4, 1),
                            pltpu.roll(y, HEAD_DIM // 4, 1))
        return y * cos + partner * sin

    def emit(pos_fn):
        q_heads_per_ref = q0_ref.shape[1] // HEAD_DIM
        for r, q_ref in enumerate((q0_ref, q1_ref, q2_ref)):
            for h in range(q_heads_per_ref):
                sl = slice(h * HEAD_DIM, (h + 1) * HEAD_DIM)
                y = _rms(q_ref[:, sl], qn_ref[...])
                col = (r * q_heads_per_ref + h) * HEAD_DIM
                qo_ref[:, col:col + HEAD_DIM] = pos_fn(y).astype(qo_ref.dtype)
        for h in range(k_ref.shape[1] // HEAD_DIM):
            sl = slice(h * HEAD_DIM, (h + 1) * HEAD_DIM)
            y = _rms(k_ref[:, sl], kn_ref[...])
            ks_ref[:, sl] = y
            ko_ref[:, sl] = pos_fn(y).astype(ko_ref.dtype)

    pl.when(pl.program_id(0) < n_plain)(functools.partial(emit, lambda y: y))
    pl.when(pl.program_id(0) >= n_plain)(functools.partial(emit, rope))


def _qk_prep(u, cos, sin, q_norm, k_norm, *, pool_w, q_w, kv_w, bt, n_plain_rows):
    T = u.shape[0]
    assert q_w == 3 * kv_w and pool_w == kv_w and n_plain_rows % bt == 0
    cb = kv_w
    row = lambda i: (i, 0)
    in_specs = [pl.BlockSpec((bt, cb), lambda i, j=j: (i, j)) for j in (1, 2, 3, 4)]
    in_specs += [pl.BlockSpec((bt, HEAD_DIM), row), pl.BlockSpec((bt, HEAD_DIM), row),
                 pl.BlockSpec((1, HEAD_DIM), lambda i: (0, 0)), pl.BlockSpec((1, HEAD_DIM), lambda i: (0, 0))]
    return pl.pallas_call(
        functools.partial(_qk_prep_kernel, n_plain=n_plain_rows // bt), grid=(T // bt,), in_specs=in_specs,
        out_specs=[pl.BlockSpec((bt, q_w), row), pl.BlockSpec((bt, kv_w), row), pl.BlockSpec((bt, kv_w), row)],
        out_shape=[jax.ShapeDtypeStruct((T, q_w), BF16), jax.ShapeDtypeStruct((T, kv_w), BF16),
                   jax.ShapeDtypeStruct((T, kv_w), F32)],
        compiler_params=_cparams(("parallel",)), name="qk_norm_rope",
    )(u, u, u, u, cos, sin, q_norm, k_norm)


def _attn_kernel(q_ref, k_ref, v_ref, o_ref, *, heads, groups, scale):
    for h in range(heads):
        kv_sl = slice(h * HEAD_DIM, (h + 1) * HEAD_DIM)
        k = k_ref[:, kv_sl].astype(BF16)
        v = v_ref[:, kv_sl].astype(BF16)
        for g in range(groups):
            c = (h * groups + g) * HEAD_DIM
            q = q_ref[:, c:c + HEAD_DIM].astype(BF16)
            s = lax.dot_general(q, k, (((1,), (1,)), ((), ())), preferred_element_type=F32) * scale
            p = jnp.exp2(s - jnp.max(s, axis=-1, keepdims=True))
            l = jnp.sum(p, axis=-1, keepdims=True)
            o = jnp.dot(p.astype(BF16), v, preferred_element_type=F32) / l
            o_ref[:, c:c + HEAD_DIM] = o.astype(o_ref.dtype)


def _attention(q, k, v, *, batch, s_len, t_len, kv_heads, groups, bq, hp, q_row0=0, q_col0=0,
               k_row0=0, k_col0=0, v_col0=0):
    qw, kw = hp * groups * HEAD_DIM, hp * HEAD_DIM
    nq = s_len // bq
    assert q_row0 % bq == 0 and q_col0 % qw == 0 and k_row0 % t_len == 0 and kv_heads % hp == 0
    assert k_col0 % kw == 0 and v_col0 % kw == 0
    qr, qc, kr, kc, vc = q_row0 // bq, q_col0 // qw, k_row0 // t_len, k_col0 // kw, v_col0 // kw
    return pl.pallas_call(
        functools.partial(_attn_kernel, heads=hp, groups=groups, scale=HEAD_DIM ** -0.5 * LOG2E),
        grid=(batch, kv_heads // hp, nq),
        in_specs=[pl.BlockSpec((bq, qw), lambda b, h, i: (qr + b * nq + i, qc + h)),
                  pl.BlockSpec((t_len, kw), lambda b, h, i: (kr + b, kc + h)),
                  pl.BlockSpec((t_len, kw), lambda b, h, i: (kr + b, vc + h))],
        out_specs=pl.BlockSpec((bq, qw), lambda b, h, i: (b * nq + i, h)),
        out_shape=jax.ShapeDtypeStruct((batch * s_len, kv_heads * groups * HEAD_DIM), BF16),
        compiler_params=_cparams(("parallel", "parallel", "arbitrary")), name="attention",
    )(q, k, v)


def _pool_kernel(u_ref, w_ref, s_ref, o_ref, pad_ref, *, seq):
    g = pl.program_id(1)
    x = u_ref[...]
    pad_ref[0:POOL_HALO, :] = jnp.zeros((POOL_HALO, x.shape[1]), F32)
    pad_ref[POOL_HALO + seq:, :] = jnp.zeros((POOL_HALO, x.shape[1]), F32)
    pad_ref[POOL_HALO:POOL_HALO + seq, :] = x
    t = lax.broadcasted_iota(jnp.int32, x.shape, 0)
    for gi, win in enumerate(POOL_WINDOWS):
        @pl.when(g == gi)
        def _(win=win):
            back, fwd = win // 2, win - win // 2
            tot = pad_ref[POOL_HALO - back:POOL_HALO - back + seq, :]
            for j in range(-back + 1, fwd):
                tot = tot + pad_ref[POOL_HALO + j:POOL_HALO + j + seq, :]
            cnt = (jnp.minimum(t + fwd, seq) - jnp.maximum(t - back, 0)).astype(F32)
            diff = (tot / cnt - x).astype(BF16)
            y = jnp.dot(diff, w_ref[0], preferred_element_type=F32) * s_ref[...]
            o_ref[...] = y.astype(o_ref.dtype)


def _pool_mix(u, w_pool, pool_scale, *, n_seq, seq, row0):
    n_groups, cg, _ = w_pool.shape
    assert row0 % seq == 0 and max(POOL_WINDOWS) // 2 <= POOL_HALO and seq % 8 == 0
    r0 = row0 // seq
    return pl.pallas_call(
        functools.partial(_pool_kernel, seq=seq), grid=(n_seq, n_groups),
        in_specs=[pl.BlockSpec((seq, cg), lambda s, g: (r0 + s, g)),
                  pl.BlockSpec((1, cg, cg), lambda s, g: (g, 0, 0)),
                  pl.BlockSpec((1, cg), lambda s, g: (0, g))],
        out_specs=pl.BlockSpec((seq, cg), lambda s, g: (s, g)),
        out_shape=jax.ShapeDtypeStruct((n_seq * seq, n_groups * cg), BF16),
        scratch_shapes=[pltpu.VMEM((seq + 2 * POOL_HALO, cg), F32)],
        compiler_params=_cparams(("parallel", "arbitrary")), name="pool_mix",
    )(u, w_pool, pool_scale)


def _na_kernel(q_ref, k_ref, v_ref, kc_ref, vc_ref, t_ref, o_ref, bias_ref, *,
               rows_per_blk, key_rows, n_rows, wr, n_blk, scale):
    blk = pl.program_id(2)
    half = (key_rows - rows_per_blk) // 2
    W = GRID_W
    lane = lax.broadcasted_iota(jnp.int32, (W, 2 * W), 1)

    def build(b):
        k_start = min(max(b * rows_per_blk - half, 0), n_rows - key_rows)
        for qr in range(rows_per_blk):
            r = b * rows_per_blk + qr
            rs = min(max(r - wr // 2, 0), n_rows - wr)
            for kp in range(key_rows // 2):
                idx = [kk - r + NA_ROWS - 1 if rs <= kk < rs + wr else 2 * NA_ROWS - 1
                       for kk in (k_start + 2 * kp, k_start + 2 * kp + 1)]
                bias_ref[qr * W:(qr + 1) * W, kp * 2 * W:(kp + 1) * 2 * W] = jnp.where(
                    lane < W, t_ref[0, idx[0]], t_ref[0, idx[1]])

    pl.when(blk == 0)(functools.partial(build, 0))
    if n_blk > 2:
        pl.when(blk == 1)(functools.partial(build, 1))
    if n_blk > 1:
        pl.when(blk == n_blk - 1)(functools.partial(build, n_blk - 1))

    k_start = jnp.clip(blk * rows_per_blk - half, 0, n_rows - key_rows)
    tok0 = pl.multiple_of(k_start * W, W * 4)
    n_keys = key_rows * W
    q = q_ref[...].astype(BF16)
    kl = k_ref[pl.ds(tok0, n_keys), :].astype(BF16)
    vl = v_ref[pl.ds(tok0, n_keys), :].astype(BF16)
    kc = kc_ref[...].astype(BF16)
    vc = vc_ref[...].astype(BF16)
    dn = (((1,), (1,)), ((), ()))
    s_loc = lax.dot_general(q, kl, dn, preferred_element_type=F32) * scale + bias_ref[...]
    s_ctx = lax.dot_general(q, kc, dn, preferred_element_type=F32) * scale
    m = jnp.maximum(jnp.max(s_loc, axis=-1, keepdims=True), jnp.max(s_ctx, axis=-1, keepdims=True))
    p_loc = jnp.exp2(s_loc - m)
    p_ctx = jnp.exp2(s_ctx - m)
    l = jnp.sum(p_loc, axis=-1, keepdims=True) + jnp.sum(p_ctx, axis=-1, keepdims=True)
    o = (jnp.dot(p_ctx.astype(BF16), vc, preferred_element_type=F32)
         + jnp.dot(p_loc.astype(BF16), vl, preferred_element_type=F32)) / l
    o_ref[...] = o.astype(o_ref.dtype)


def _na_col_tables(rpb):
    H = rpb.shape[0]
    col = jnp.arange(GRID_W)
    cstart = jnp.clip(col - NA_COLS // 2, 0, GRID_W - NA_COLS)
    col_valid = (col[None, :] >= cstart[:, None]) & (col[None, :] < cstart[:, None] + NA_COLS)
    col_idx = jnp.clip(col[None, :] - col[:, None] + NA_COLS - 1, 0, 2 * NA_COLS - 2)
    onehot = (col_idx[None] == jnp.arange(2 * NA_COLS - 1)[:, None, None]).astype(F32)
    t = jnp.einsum("hrc,cqk->hrqk", rpb.astype(F32), onehot, precision=lax.Precision.HIGHEST)
    t = jnp.where(col_valid[None, None], t * LOG2E, NEG_INF)
    t = jnp.concatenate([t, jnp.full((H, 1, GRID_W, GRID_W), NEG_INF, F32)], axis=1)
    return jnp.concatenate([t, t], axis=-1)


def _na_attention(qkv, k_ctx, v_ctx, rpb, *, batch, n_tok, heads, row0, t_ctx):
    n_rows = n_tok // GRID_W
    wr = min(NA_ROWS, n_rows)
    rows_per_blk = min(NA_ROWS, n_rows)
    key_rows = min(rows_per_blk + wr, n_rows)
    n_blk = n_rows // rows_per_blk
    bq = rows_per_blk * GRID_W
    assert n_rows % rows_per_blk == 0 and row0 % n_tok == 0 and row0 % bq == 0 and key_rows % 2 == 0
    assert rows_per_blk % 4 == 0 and ((key_rows - rows_per_blk) // 2) % 4 == 0
    tables = _na_col_tables(rpb)
    qr0, kr0 = row0 // bq, row0 // n_tok
    return pl.pallas_call(
        functools.partial(_na_kernel, rows_per_blk=rows_per_blk, key_rows=key_rows, n_rows=n_rows, wr=wr,
                          n_blk=n_blk, scale=HEAD_DIM ** -0.5 * LOG2E),
        grid=(heads, batch, n_blk),
        in_specs=[pl.BlockSpec((bq, HEAD_DIM), lambda h, b, i: (qr0 + b * n_blk + i, h)),
                  pl.BlockSpec((n_tok, HEAD_DIM), lambda h, b, i: (kr0 + b, heads + h)),
                  pl.BlockSpec((n_tok, HEAD_DIM), lambda h, b, i: (kr0 + b, 2 * heads + h)),
                  pl.BlockSpec((t_ctx, HEAD_DIM), lambda h, b, i: (b, h)),
                  pl.BlockSpec((t_ctx, HEAD_DIM), lambda h, b, i: (b, h)),
                  pl.BlockSpec((1, 2 * NA_ROWS, GRID_W, 2 * GRID_W), lambda h, b, i: (h, 0, 0, 0))],
        out_specs=pl.BlockSpec((bq, HEAD_DIM), lambda h, b, i: (b * n_blk + i, h)),
        out_shape=jax.ShapeDtypeStruct((batch * n_tok, heads * HEAD_DIM), BF16),
        scratch_shapes=[pltpu.VMEM((bq, key_rows * GRID_W), F32)],
        compiler_params=_cparams(("arbitrary", "arbitrary", "arbitrary")), name="na_attention",
    )(qkv, qkv, qkv, k_ctx, v_ctx, tables)


def _row_copy(src_hbm, row, dst, slot, sem):
    return pltpu.make_async_copy(src_hbm.at[pl.ds(row, 1), :], dst.at[pl.ds(slot, 1), :], sem)


def _gather_kernel(src_ref, x_hbm, o_ref, buf, sem, *, rt, n_steps):
    i = pl.program_id(0)
    cur = i % 2

    def issue(step, s):
        def body(r, c):
            _row_copy(x_hbm, src_ref[step * rt + r], buf.at[s], r, sem.at[s]).start()
            return c
        lax.fori_loop(0, rt, body, 0, unroll=8)

    @pl.when(i == 0)
    def _():
        issue(0, 0)

    @pl.when(i + 1 < n_steps)
    def _():
        issue(i + 1, 1 - cur)

    pltpu.make_async_copy(x_hbm.at[pl.ds(0, rt), :], buf.at[cur], sem.at[cur]).wait()
    o_ref[...] = buf[cur].astype(o_ref.dtype)


def _gather_rows(x, src, *, rt):
    n = src.shape[0]
    D = x.shape[1]
    return pl.pallas_call(
        functools.partial(_gather_kernel, rt=rt, n_steps=n // rt),
        out_shape=jax.ShapeDtypeStruct((n, D), BF16),
        grid_spec=pltpu.PrefetchScalarGridSpec(
            num_scalar_prefetch=1, grid=(n // rt,),
            in_specs=[pl.BlockSpec(memory_space=pl.ANY)],
            out_specs=pl.BlockSpec((rt, D), lambda i, s: (i, 0)),
            scratch_shapes=[pltpu.VMEM((2, rt, D), F32), pltpu.SemaphoreType.DMA((2,))]),
        compiler_params=_cparams(("arbitrary",)), name="moe_dispatch",
    )(src, x)


def _combine_kernel(pos_ref, x_ref, g_ref, w_ref, fn_ref, y_hbm, oa_ref, ob_ref, buf, sem, *, ct, n_steps, n_first):
    i = pl.program_id(0)
    cur = i % 2

    def issue(step, s):
        def body(r, c):
            for kk in range(TOP_K):
                _row_copy(y_hbm, pos_ref[TOP_K * (step * ct + r) + kk], buf.at[s, kk], r, sem.at[s]).start()
            return c
        lax.fori_loop(0, ct, body, 0, unroll=8)

    @pl.when(i == 0)
    def _():
        issue(0, 0)

    @pl.when(i + 1 < n_steps)
    def _():
        issue(i + 1, 1 - cur)

    for kk in range(TOP_K):
        pltpu.make_async_copy(y_hbm.at[pl.ds(0, ct), :], buf.at[cur, kk], sem.at[cur]).wait()
    w = w_ref[...]
    acc = w[:, 0:1] * buf[cur, 0]
    for kk in range(1, TOP_K):
        acc = acc + w[:, kk:kk + 1] * buf[cur, kk]
    out = _rms(x_ref[...] + g_ref[0] * acc, fn_ref[...])

    @pl.when(i < n_first)
    def _():
        oa_ref[...] = out

    @pl.when(i >= n_first)
    def _():
        ob_ref[...] = out


def _combine_norm(x, gate, top_w, final_g, y, pos, *, ct, gate_rows, rows_a):
    T, D = x.shape
    n_first = rows_a // ct
    assert n_first * ct == rows_a and 0 < n_first < T // ct
    return pl.pallas_call(
        functools.partial(_combine_kernel, ct=ct, n_steps=T // ct, n_first=n_first),
        out_shape=[jax.ShapeDtypeStruct((rows_a, D), F32), jax.ShapeDtypeStruct((T - rows_a, D), F32)],
        grid_spec=pltpu.PrefetchScalarGridSpec(
            num_scalar_prefetch=1, grid=(T // ct,),
            in_specs=[pl.BlockSpec((ct, D), lambda i, p: (i, 0)),
                      pl.BlockSpec((1, 1, D), lambda i, p: ((i * ct) // gate_rows, 0, 0)),
                      pl.BlockSpec((ct, LANES), lambda i, p: (i, 0)),
                      pl.BlockSpec((1, D), lambda i, p: (0, 0)),
                      pl.BlockSpec(memory_space=pl.ANY)],
            out_specs=[pl.BlockSpec((ct, D), lambda i, p: (jnp.minimum(i, n_first - 1), 0)),
                       pl.BlockSpec((ct, D), lambda i, p: (jnp.maximum(i - n_first, 0), 0))],
            scratch_shapes=[pltpu.VMEM((2, TOP_K, ct, D), F32), pltpu.SemaphoreType.DMA((2,))]),
        compiler_params=_cparams(("arbitrary",)), name="moe_combine",
    )(pos, x, gate, top_w, final_g, y)


def _routing_plan(top_i, n_experts, tm):
    T = top_i.shape[0]
    e_flat = top_i.reshape(-1)
    onehot = (e_flat[:, None] == jnp.arange(n_experts)[None, :]).astype(jnp.int32)
    counts = jnp.sum(onehot, axis=0)
    rank = jnp.sum((jnp.cumsum(onehot, axis=0) - onehot) * onehot, axis=1)
    gsz = ((counts + tm - 1) // tm) * tm
    gend = jnp.cumsum(gsz)
    goff = gend - gsz
    pos = (goff[e_flat] + rank).astype(jnp.int32)
    n_tiles = (T * TOP_K) // tm + n_experts
    src = jnp.zeros((n_tiles * tm,), jnp.int32).at[pos].set(jnp.arange(T * TOP_K, dtype=jnp.int32) // TOP_K)
    n_valid = (gend[-1] // tm).astype(jnp.int32)
    tile_row = jnp.minimum(jnp.arange(n_tiles, dtype=jnp.int32), n_valid - 1) * tm
    tile_expert = jnp.sum((tile_row[:, None] >= gend[None, :]).astype(jnp.int32), axis=1)
    tile_rows = jnp.clip((goff + counts)[tile_expert] - tile_row, 0, tm)
    return pos, src, tile_expert.astype(jnp.int32), n_valid.reshape(1), tile_rows.astype(jnp.int32)


def _pick(n, prefs):
    for p in prefs:
        if n % p == 0:
            return p
    return n


def _modulation(cvecs, w, b):
    n_c, D = cvecs.shape
    rows = 8
    a = jnp.zeros((rows, D), BF16).at[:n_c].set(jax.nn.silu(cvecs).astype(BF16))
    m = _matmul(a, (w[None],), bm=rows, bn=_pick(6 * D, (1024, 512, 256, 128)), bk=D,
                mode="bias", extra=(b[None, :],), name="modulation")
    return jnp.transpose(m[:n_c].reshape(n_c, 6, 1, D), (1, 0, 2, 3))


def _rope_tables(n_prompt_rows, dec_batch, dec_seq):
    t = jnp.arange(dec_seq)
    pos = jnp.stack([t // GRID_W, t % GRID_W], axis=-1).astype(F32)
    n_freq = HEAD_DIM // 4
    inv = ROPE_THETA ** (-jnp.arange(n_freq, dtype=F32) / n_freq)
    ang = pos[:, :, None] * inv
    cos = jnp.concatenate([jnp.cos(ang)] * 2, axis=-1).reshape(dec_seq, HEAD_DIM)
    sin = jnp.concatenate([-jnp.sin(ang), jnp.sin(ang)], axis=-1).reshape(dec_seq, HEAD_DIM)
    cos = jnp.concatenate([jnp.ones((n_prompt_rows, HEAD_DIM), F32)] + [cos] * dec_batch)
    sin = jnp.concatenate([jnp.zeros((n_prompt_rows, HEAD_DIM), F32)] + [sin] * dec_batch)
    return cos, sin


def kernel(x_prompt, x_sample, c, c_ctx, cache_l0_attn_k, cache_l0_attn_v, cache_l1_na_k, cache_l1_na_v,
           l0_ada_w, l0_ada_b, l0_norm1, l0_w_in, l0_q_norm, l0_k_norm, l0_w_pool, l0_pool_scale, l0_w_out,
           l0_norm2, l0_ffn_w1, l0_ffn_w3, l0_ffn_w2,
           l1_ada_w, l1_ada_b, l1_norm1, l1_w_qkv, l1_rpb, l1_w_out, l1_norm2, l1_router,
           l1_exp_w1, l1_exp_w3, l1_exp_w2, final_norm):
    batch, seq, D = x_prompt.shape
    dec_batch, dec_seq, _ = x_sample.shape
    past_len = cache_l0_attn_k.shape[1]
    kv_heads0 = cache_l0_attn_k.shape[2]
    heads1 = cache_l1_na_k.shape[2]
    n_pool, cg, _ = l0_w_pool.shape
    pool_w = n_pool * cg
    kv_w = kv_heads0 * HEAD_DIM
    q_w = l0_w_in.shape[1] - pool_w - 2 * kv_w
    groups0 = q_w // kv_w
    c_w = heads1 * HEAD_DIM
    n_experts = l1_router.shape[1]
    Tp, Ts = batch * seq, dec_batch * dec_seq
    T = Tp + Ts
    group_rows = math.gcd(Tp, dec_seq)
    grp_to_c = jnp.array([0] * (Tp // group_rows) + [1 + b for b in range(dec_batch)
                                                      for _ in range(dec_seq // group_rows)], jnp.int32)

    bt = _pick(group_rows, (256, 128, 64, 32, 16, 8))
    bm = _pick(group_rows, (1024, 512, 256, 128, 64, 32, 16, 8))
    bm2 = _pick(group_rows, (512, 256, 128, 64, 32, 16, 8))
    blk = lambda n: _pick(n, (512, 256, 128))
    seq_blk = lambda n: _pick(n, (256, 128, 64, 32, 16, 8))
    mods = lambda m: [m[i][grp_to_c] for i in range(6)]
    cvecs = jnp.concatenate([c_ctx[None, :], c], axis=0)
    gg = dict(gate_rows=group_rows)

    xa, xb = x_prompt.reshape(Tp, D), x_sample.reshape(Ts, D)

    sh1, sc1, g1, sh2, sc2, g2 = mods(_modulation(cvecs, l0_ada_w, l0_ada_b))
    h = _adaln_stacked(xa, xb, l0_norm1[None], sc1, sh1, group_rows=group_rows, bt=bt, out_dtype=BF16)
    bn_in = blk(l0_w_in.shape[1])
    u, ffn_w2 = _matmul_with_cast(h, (l0_w_in[None],), l0_ffn_w2, bm=bm, bn=bn_in, bk=D, name="l0_in_proj")
    cos, sin = _rope_tables(Tp, dec_batch, dec_seq)
    q_att, k_att, k_state = _qk_prep(u, cos, sin, l0_q_norm[None], l0_k_norm[None],
                                     pool_w=pool_w, q_w=q_w, kv_w=kv_w, bt=bt, n_plain_rows=Tp)
    v_col0 = pool_w + q_w + kv_w
    att_p = _attention(q_att, k_att, u, batch=batch, s_len=seq, t_len=seq, kv_heads=kv_heads0, groups=groups0,
                       bq=seq_blk(seq), hp=kv_heads0, v_col0=v_col0)
    k_lat = jnp.concatenate([cache_l0_attn_k.reshape(dec_batch, past_len, kv_w).astype(BF16),
                             k_att[Tp:].reshape(dec_batch, dec_seq, kv_w)], axis=1).reshape(-1, kv_w)
    v_lat = jnp.concatenate([cache_l0_attn_v.reshape(dec_batch, past_len, kv_w),
                             u[Tp:, v_col0:].reshape(dec_batch, dec_seq, kv_w)], axis=1).reshape(-1, kv_w)
    att_s = _attention(q_att, k_lat, v_lat, batch=dec_batch, s_len=dec_seq, t_len=past_len + dec_seq,
                       kv_heads=kv_heads0, groups=groups0, bq=seq_blk(dec_seq), hp=1, q_row0=Tp)
    w_pool = l0_w_pool.astype(BF16)
    pool_p = _pool_mix(u, w_pool, l0_pool_scale[None], n_seq=batch, seq=seq, row0=0)
    pool_s = _pool_mix(u, w_pool, l0_pool_scale[None], n_seq=dec_batch, seq=dec_seq, row0=Tp)
    x = _matmul([(pool_p, pool_s), (att_p, att_s)], (l0_w_out[None],), bm=bm2, bn=blk(D), bk=pool_w + q_w,
                mode="gated_res", extra=(xa, xb, g1), n_first=Tp // bm2, name="l0_out_proj", **gg)
    h = _adaln(x, l0_norm2[None], sc2, sh2, group_rows=group_rows, bt=bt, out_dtype=BF16)
    d_ff = l0_ffn_w1.shape[1]
    d_fe = l1_exp_w1.shape[2]
    hh, exp_w2 = _matmul_with_cast(h, (l0_ffn_w1[None], l0_ffn_w3[None]), l1_exp_w2.reshape(n_experts * d_fe, D),
                                   bm=bm, bn=blk(d_ff), bk=D, mode="swiglu", out_dtype=BF16, name="l0_ffn_up")
    x = _matmul(hh, (ffn_w2[None],), bm=bm, bn=blk(D),
                bk=_pick(d_ff, (5504, 2048, 1024, 512, 256, 128)),
                mode="gated_res", extra=(x, g2), name="l0_ffn_down", **gg)
    state_l0_k = k_state[:Tp].reshape(batch, seq, kv_heads0, HEAD_DIM)
    state_l0_v = u[:Tp, v_col0:].reshape(batch, seq, kv_heads0, HEAD_DIM)

    sh1, sc1, g1, sh2, sc2, g2 = mods(_modulation(cvecs, l1_ada_w, l1_ada_b))
    h = _adaln(x, l1_norm1[None], sc1, sh1, group_rows=group_rows, bt=bt, out_dtype=BF16)
    qkv = _matmul(h, (l1_w_qkv[None],), bm=bm, bn=blk(3 * c_w), bk=D, name="l1_qkv_proj")
    att_p = _attention(qkv, qkv, qkv, batch=batch, s_len=seq, t_len=seq, kv_heads=heads1, groups=1,
                       bq=seq_blk(seq), hp=_pick(heads1, (8, 4, 2, 1)), k_col0=c_w, v_col0=2 * c_w)
    att_s = _na_attention(qkv, cache_l1_na_k.reshape(dec_batch * past_len, c_w),
                          cache_l1_na_v.reshape(dec_batch * past_len, c_w), l1_rpb,
                          batch=dec_batch, n_tok=dec_seq, heads=heads1, row0=Tp, t_ctx=past_len)
    x = _matmul([(att_p, att_s)], (l1_w_out[None],), bm=bm2, bn=blk(D), bk=c_w,
                mode="gated_res", extra=(x, g1), n_first=Tp // bm2, name="l1_out_proj", **gg)
    h32, top_i, top_w = _adaln(x, l1_norm2[None], sc2, sh2, group_rows=group_rows, bt=bt, out_dtype=F32,
                               router=l1_router)
    tm = _pick(T * TOP_K, (512, 256, 128, 64, 32, 16, 8))
    sub = LANES if tm % LANES == 0 else tm
    pos, src, tile_expert, n_valid, tile_rows = _routing_plan(top_i[:, :TOP_K], n_experts, tm)
    routed = dict(tile_group=tile_expert, n_valid=n_valid, tile_rows=tile_rows, sub=sub)
    xs = _gather_rows(h32, src, rt=_pick(tm, (256, 128, 64, 32, 16, 8)))
    he = _moe_up(xs, l1_exp_w1, l1_exp_w3, bm=tm, bn=blk(d_fe), **routed)
    ye = _matmul(he, (exp_w2.reshape(n_experts, d_fe, D),), bm=tm, bn=_pick(D, (1024, 512, 256, 128)),
                 bk=_pick(d_fe, (7168, 3584, 2048, 1024, 512, 256, 128)), name="moe_down", **routed)
    y_p, y_s = _combine_norm(x, g2, top_w, final_norm[None], ye, pos, ct=_pick(group_rows, (128, 64, 32, 16, 8)),
                             gate_rows=group_rows, rows_a=Tp)
    state_l1_k = qkv[:Tp, c_w:2 * c_w].reshape(batch, seq, heads1, HEAD_DIM)
    state_l1_v = qkv[:Tp, 2 * c_w:].reshape(batch, seq, heads1, HEAD_DIM)

    return (y_p.reshape(batch, seq, D), y_s.reshape(dec_batch, dec_seq, D),
            state_l0_k, state_l0_v, state_l1_k, state_l1_v)
```
